```python
import math
import jax, jax.numpy as jnp
from jax import lax
import numpy as np

D_MODEL = 2048
BATCH = 4
SEQ = 2048
DEPTH = 1
DEC_BATCH = 128
DEC_SEQ = 8
PAST_LEN = 16384
PAGE_SIZE = 128

MIX_WIDTH = D_MODEL
CONV_CH = MIX_WIDTH // 2
CONV_KERNEL = 31
GLA_HEADS = 4
GLA_DV = (MIX_WIDTH - CONV_CH) // GLA_HEADS
GLA_DK = GLA_DV // 2
GLA_QK = GLA_HEADS * GLA_DK
GLA_V = GLA_HEADS * GLA_DV
GLA_RANK = 16
GLA_GATE_NORM = 16.0
GLA_CHUNK = 64
PROJ_DIM = 2 * CONV_CH + 2 * GLA_QK + 2 * GLA_V + GLA_RANK
N_EXPERTS = 32
TOP_K = 4
D_FF = D_MODEL
SWIGLU_LIMIT = 7.0
SWIGLU_ALPHA = 1.702
MOE_BLOCK = 128
EPS = 1e-5

kernel_name = "hymba_conformer_gla_moe_step"


def _rmsnorm(x, g):
    xf = x.astype(jnp.float32)
    y = xf * lax.rsqrt(jnp.mean(xf * xf, axis=-1, keepdims=True) + EPS)
    return (y * g.astype(jnp.float32)).astype(x.dtype)


def _layernorm(x, g, b):
    xf = x.astype(jnp.float32)
    mu = jnp.mean(xf, axis=-1, keepdims=True)
    var = jnp.mean(jnp.square(xf - mu), axis=-1, keepdims=True)
    y = (xf - mu) * lax.rsqrt(var + EPS)
    return (y * g.astype(jnp.float32) + b.astype(jnp.float32)).astype(x.dtype)


def _conv_mixer(u_a, u_b, conv_buf, w_dw, b_dw, ln_g, ln_b):
    u = u_a * jax.nn.sigmoid(u_b)
    full = jnp.concatenate([conv_buf.astype(u.dtype), u], axis=1)
    y = lax.conv_general_dilated(full, w_dw[:, None, :].astype(u.dtype), (1,), 'VALID',
                                 dimension_numbers=('NWC', 'WIO', 'NWC'),
                                 feature_group_count=CONV_CH) + b_dw
    new_buf = full[:, -(CONV_KERNEL - 1):, :]
    y = jax.nn.silu(_layernorm(y, ln_g, ln_b))
    return y, new_buf


def _to_chunks(t, n, c):
    b, _, h, d = t.shape
    return t.reshape(b, n, c, h, d).transpose(1, 0, 3, 2, 4)


def _gla_mixer(q, k, v, g, a_low, state0, w_alpha, b_alpha, gn_g):
    bsz, t_len, _ = q.shape
    f32 = jnp.float32
    log_a = jax.nn.log_sigmoid((a_low @ w_alpha + b_alpha).astype(f32)) / GLA_GATE_NORM
    qh = q.astype(f32).reshape(bsz, t_len, GLA_HEADS, GLA_DK) * (GLA_DK ** -0.5)
    kh = k.astype(f32).reshape(bsz, t_len, GLA_HEADS, GLA_DK)
    vh = v.astype(f32).reshape(bsz, t_len, GLA_HEADS, GLA_DV)
    lh = log_a.reshape(bsz, t_len, GLA_HEADS, GLA_DK)
    c = math.gcd(t_len, GLA_CHUNK)
    n = t_len // c
    mask = jnp.tril(jnp.ones((c, c), dtype=bool))

    def step(s, inp):
        qc, kc, vc, lc = inp
        bcum = jnp.cumsum(lc, axis=2)
        o_inter = jnp.einsum('bhik,bhkv->bhiv', qc * jnp.exp(bcum), s)
        diff = jnp.where(mask[:, :, None],
                         bcum[:, :, :, None, :] - bcum[:, :, None, :, :], -jnp.inf)
        att = jnp.einsum('bhik,bhjk,bhijk->bhij', qc, kc, jnp.exp(diff))
        o_intra = jnp.einsum('bhij,bhjv->bhiv', att, vc)
        b_last = bcum[:, :, -1:, :]
        s_new = jnp.exp(b_last[:, :, 0, :])[..., None] * s + jnp.einsum(
            'bhck,bhcv->bhkv', kc * jnp.exp(b_last - bcum), vc)
        return s_new, o_inter + o_intra

    s_fin, o = lax.scan(step, state0.astype(f32),
                        (_to_chunks(qh, n, c), _to_chunks(kh, n, c),
                         _to_chunks(vh, n, c), _to_chunks(lh, n, c)))
    o = o.transpose(1, 0, 3, 2, 4).reshape(bsz, t_len, GLA_HEADS, GLA_DV)
    o = o * lax.rsqrt(jnp.mean(o * o, axis=-1, keepdims=True) + EPS) * gn_g.astype(f32)
    gh = g.astype(f32).reshape(bsz, t_len, GLA_HEADS, GLA_DV)
    o = (o * jax.nn.silu(gh)).reshape(bsz, t_len, GLA_V)
    return o.astype(q.dtype), s_fin


def _moe(h, w_router, b_router, w_gate, b_gate, w_up, b_up, w_down, b_down):
    bsz, t_len, d = h.shape
    xt = h.reshape(-1, d)
    n_tok = xt.shape[0]
    logits = (xt @ w_router + b_router).astype(jnp.float32)
    top_v, top_i = lax.top_k(logits, TOP_K)
    top_w = jax.nn.softmax(top_v, axis=-1)
    n_asg = n_tok * TOP_K
    e_flat = top_i.reshape(-1)
    tok_flat = jnp.repeat(jnp.arange(n_tok, dtype=jnp.int32), TOP_K)
    w_flat = top_w.reshape(-1)
    order = jnp.argsort(e_flat)
    e_s, tok_s, w_s = e_flat[order], tok_flat[order], w_flat[order]
    counts = jnp.bincount(e_flat, length=N_EXPERTS)
    blocks_per = (counts + MOE_BLOCK - 1) // MOE_BLOCK
    blk_end = jnp.cumsum(blocks_per)
    blk_start = blk_end - blocks_per
    grp_start = jnp.cumsum(counts) - counts
    rank = jnp.arange(n_asg) - grp_start[e_s]
    slot = blk_start[e_s] * MOE_BLOCK + rank
    n_blocks = -(-n_asg // MOE_BLOCK) + N_EXPERTS
    n_slots = n_blocks * MOE_BLOCK
    slot_tok = jnp.zeros((n_slots,), jnp.int32).at[slot].set(tok_s)
    slot_w = jnp.zeros((n_slots,), jnp.float32).at[slot].set(w_s)
    blk_expert = jnp.minimum(jnp.searchsorted(blk_end, jnp.arange(n_blocks), side='right'),
                             N_EXPERTS - 1)
    xb = xt[slot_tok].reshape(n_blocks, MOE_BLOCK, d)

    def run(args):
        xblk, e = args
        gate = jnp.minimum(xblk @ w_gate[e] + b_gate[e], SWIGLU_LIMIT)
        up = jnp.clip(xblk @ w_up[e] + b_up[e], -SWIGLU_LIMIT, SWIGLU_LIMIT)
        glu = gate * jax.nn.sigmoid(SWIGLU_ALPHA * gate)
        return ((up + 1.0) * glu) @ w_down[e] + b_down[e]

    yb = lax.map(run, (xb, blk_expert)).reshape(n_slots, d)
    y = jnp.zeros((n_tok, d), jnp.float32).at[slot_tok].add(
        yb.astype(jnp.float32) * slot_w[:, None])
    return y.astype(h.dtype).reshape(bsz, t_len, d)


def _layer(x, c, conv_buf, gla_state, w_ada, b_ada, norm1_g, w_in, w_dw, b_dw,
           conv_ln_g, conv_ln_b, w_alpha, b_alpha, gla_norm_g, w_out, norm2_g,
           w_router, b_router, w_gate, b_gate, w_up, b_up, w_down, b_down):
    mod = jax.nn.silu(c) @ w_ada + b_ada
    sh1, sc1, g1, sh2, sc2, g2 = [m[:, None, :] for m in jnp.split(mod, 6, axis=-1)]
    h = _rmsnorm(x, norm1_g) * (1.0 + sc1) + sh1
    u = h @ w_in
    cuts = np.cumsum([CONV_CH, CONV_CH, GLA_QK, GLA_QK, GLA_V, GLA_V]).tolist()
    u_a, u_b, q, k, v, g, a_low = jnp.split(u, cuts, axis=-1)
    y_conv, new_buf = _conv_mixer(u_a, u_b, conv_buf, w_dw, b_dw, conv_ln_g, conv_ln_b)
    y_gla, new_state = _gla_mixer(q, k, v, g, a_low, gla_state, w_alpha, b_alpha, gla_norm_g)
    x = x + g1 * (jnp.concatenate([y_conv, y_gla], axis=-1) @ w_out)
    h2 = _rmsnorm(x, norm2_g) * (1.0 + sc2) + sh2
    x = x + g2 * _moe(h2, w_router, b_router, w_gate, b_gate, w_up, b_up, w_down, b_down)
    return x, new_buf, new_state


def setup_inputs(seed: int = 0) -> dict:
    key = jax.random.key(seed)
    ks = jax.random.split(key, 32)
    f32 = jnp.float32

    def nrm(k, shape, scale):
        return jax.random.normal(k, shape, f32) * scale

    L, D, E, F = DEPTH, D_MODEL, N_EXPERTS, D_FF
    return {
        "x_prompt": nrm(ks[0], (BATCH, SEQ, D), 1.0),
        "x_sample": nrm(ks[1], (DEC_BATCH, DEC_SEQ, D), 1.0),
        "c_prompt": nrm(ks[2], (BATCH, D), 1.0),
        "c_sample": nrm(ks[3], (DEC_BATCH, D), 1.0),
        "state_conv": nrm(ks[4], (L, DEC_BATCH, CONV_KERNEL - 1, CONV_CH), 0.5),
        "state_gla": nrm(ks[5], (L, DEC_BATCH, GLA_HEADS, GLA_DK, GLA_DV), 0.5),
        "w_ada": nrm(ks[6], (L, D, 6 * D), 0.5 * D ** -0.5),
        "b_ada": nrm(ks[7], (L, 6 * D), 0.02),
        "norm1_g": 1.0 + nrm(ks[8], (L, D), 0.02),
        "w_in": nrm(ks[9], (L, D, PROJ_DIM), D ** -0.5),
        "w_dw": nrm(ks[10], (L, CONV_KERNEL, CONV_CH), CONV_KERNEL ** -0.5),
        "b_dw": nrm(ks[11], (L, CONV_CH), 0.02),
        "conv_ln_g": 1.0 + nrm(ks[12], (L, CONV_CH), 0.02),
        "conv_ln_b": nrm(ks[13], (L, CONV_CH), 0.02),
        "w_alpha": nrm(ks[14], (L, GLA_RANK, GLA_QK), GLA_RANK ** -0.5),
        "b_alpha": nrm(ks[15], (L, GLA_QK), 0.1),
        "gla_norm_g": 1.0 + nrm(ks[16], (L, GLA_DV), 0.02),
        "w_out": nrm(ks[17], (L, MIX_WIDTH, D), MIX_WIDTH ** -0.5),
        "norm2_g": 1.0 + nrm(ks[18], (L, D), 0.02),
        "w_router": nrm(ks[19], (L, D, E), D ** -0.5),
        "b_router": nrm(ks[20], (L, E), 0.01),
        "w_gate": nrm(ks[21], (L, E, D, F), D ** -0.5),
        "b_gate": nrm(ks[22], (L, E, F), 0.02),
        "w_up": nrm(ks[23], (L, E, D, F), D ** -0.5),
        "b_up": nrm(ks[24], (L, E, F), 0.02),
        "w_down": nrm(ks[25], (L, E, F, D), F ** -0.5),
        "b_down": nrm(ks[26], (L, E, D), 0.02),
        "final_norm_g": 1.0 + nrm(ks[27], (D,), 0.02),
    }


def reference(x_prompt, x_sample, c_prompt, c_sample, state_conv, state_gla,
              w_ada, b_ada, norm1_g, w_in, w_dw, b_dw, conv_ln_g, conv_ln_b,
              w_alpha, b_alpha, gla_norm_g, w_out, norm2_g, w_router, b_router,
              w_gate, b_gate, w_up, b_up, w_down, b_down, final_norm_g):
    xp, xs = x_prompt, x_sample
    conv_p, gla_p, conv_s, gla_s = [], [], [], []
    for l in range(DEPTH):
        lw = (w_ada[l], b_ada[l], norm1_g[l], w_in[l], w_dw[l], b_dw[l], conv_ln_g[l],
              conv_ln_b[l], w_alpha[l], b_alpha[l], gla_norm_g[l], w_out[l], norm2_g[l],
              w_router[l], b_router[l], w_gate[l], b_gate[l], w_up[l], b_up[l],
              w_down[l], b_down[l])
        buf0 = jnp.zeros((xp.shape[0], CONV_KERNEL - 1, CONV_CH), xp.dtype)
        st0 = jnp.zeros((xp.shape[0], GLA_HEADS, GLA_DK, GLA_DV), jnp.float32)
        xp, nb_p, ns_p = _layer(xp, c_prompt, buf0, st0, *lw)
        xs, nb_s, ns_s = _layer(xs, c_sample, state_conv[l], state_gla[l], *lw)
        conv_p.append(nb_p)
        gla_p.append(ns_p.astype(x_prompt.dtype))
        conv_s.append(nb_s.astype(state_conv.dtype))
        gla_s.append(ns_s.astype(state_gla.dtype))
    y_prompt = _rmsnorm(xp, final_norm_g)
    y_sample = _rmsnorm(xs, final_norm_g)
    return (y_prompt, y_sample, jnp.stack(conv_p, 0), jnp.stack(gla_p, 0),
            jnp.stack(conv_s, 0), jnp.stack(gla_s, 0))
```

```python
import functools

import numpy as np
import jax
import jax.numpy as jnp
from jax import lax
from jax.experimental import pallas as pl
from jax.experimental.pallas import tpu as pltpu

F32 = jnp.float32
BF16 = jnp.bfloat16
I32 = jnp.int32

EPS = 1e-5
GLA_HEADS = 4
GLA_GATE_NORM = 16.0
GLA_CHUNK = 64
TOP_K = 4
SWIGLU_LIMIT = 7.0
SWIGLU_ALPHA = 1.702

V7X_VMEM_BYTES = 64 * 1024 * 1024
VMEM_CAP_BYTES = V7X_VMEM_BYTES - 6 * 1024 * 1024
HI16 = -65536


def _params(semantics, vmem_bytes):
    limit = int(min(VMEM_CAP_BYTES, max(vmem_bytes * 5 // 4 + (4 << 20), 16 << 20)))
    return pltpu.CompilerParams(dimension_semantics=semantics, vmem_limit_bytes=limit)


def _tile(n, want):
    t = min(want, n)
    while n % t or t % 8:
        t -= 8 if t % 8 == 0 else t % 8
        assert t > 0, (n, want)
    return t


def _dot(a, b):
    return jnp.dot(a, b, preferred_element_type=F32)


def _split3(v):
    hi = v.astype(BF16)
    r = v - hi.astype(F32)
    mid = r.astype(BF16)
    lo = (r - mid.astype(F32)).astype(BF16)
    return hi, mid, lo


def _dot_sel(m01, v):
    hi, mid, lo = _split3(v)
    return _dot(m01, hi) + _dot(m01, mid) + _dot(m01, lo)


def _sigmoid(x):
    return jax.nn.sigmoid(x)


def _rms(x, g):
    return x * lax.rsqrt(jnp.mean(x * x, axis=-1, keepdims=True) + EPS) * g


def _expand_rows(m_ref, i, tm, seq_len):
    if seq_len >= tm:
        return m_ref[pl.ds((i * tm) // seq_len, 1), :]
    nseq = tm // seq_len
    r = lax.broadcasted_iota(I32, (tm, nseq), 0)
    c = lax.broadcasted_iota(I32, (tm, nseq), 1) * seq_len
    sel = ((r >= c) & (r < c + seq_len)).astype(BF16)
    return _dot_sel(sel, m_ref[...])


def _table_spec(n_seq, d, tm, seq_len):
    if seq_len >= tm:
        return pl.BlockSpec((n_seq, d), lambda i, *_: (0, 0))
    return pl.BlockSpec((tm // seq_len, d), lambda i, *_: (i, 0))


def _ada_kernel(c_ref, w_ref, b_ref, o_ref):
    c = c_ref[...]
    s = (c * _sigmoid(c)).astype(BF16)
    o_ref[...] = _dot(s, w_ref[...].astype(BF16)) + b_ref[...]


def _ada(c_all, w_ada, b_ada):
    bc, d = c_all.shape
    n6 = w_ada.shape[1]
    tn = min(1024, n6)
    return pl.pallas_call(
        _ada_kernel,
        grid=(n6 // tn,),
        in_specs=[pl.BlockSpec((bc, d), lambda j: (0, 0)),
                  pl.BlockSpec((d, tn), lambda j: (0, j)),
                  pl.BlockSpec((1, tn), lambda j: (0, j))],
        out_specs=pl.BlockSpec((bc, tn), lambda j: (0, j)),
        out_shape=jax.ShapeDtypeStruct((bc, n6), F32),
        compiler_params=_params(("arbitrary",), 2 * d * tn * 4 + d * tn * 2 + 4 * bc * (d + tn) * 4),
        name="ada",
    )(c_all, w_ada, b_ada.reshape(1, n6))


def _in_kernel(x_ref, sh_ref, sc_ref, g_ref, w_ref, wlow_ref, wal_ref, bal_ref, u_ref, la_ref, h_scr,
               *, tm, seq_len):
    i = pl.program_id(0)

    @pl.when(pl.program_id(1) == 0)
    def _():
        xn = _rms(x_ref[...], g_ref[...])
        h = xn * (1.0 + _expand_rows(sc_ref, i, tm, seq_len)) + _expand_rows(sh_ref, i, tm, seq_len)
        hb = h.astype(BF16)
        h_scr[...] = hb
        a_low = _dot(hb, wlow_ref[...].astype(BF16))
        z = _dot(a_low.astype(BF16), wal_ref[...].astype(BF16)) + bal_ref[...]
        la_ref[...] = (jnp.minimum(z, 0.0) - jnp.log(1.0 + jnp.exp(-jnp.abs(z)))) * (1.0 / GLA_GATE_NORM)

    u_ref[...] = _dot(h_scr[...], w_ref[...].astype(BF16))


def _in_proj(x2, sh, sc, g, w_in, w_low, w_alpha, b_alpha, seq_len, n_main):
    n, d = x2.shape
    n_seq = sh.shape[0]
    qk = w_alpha.shape[1]
    rank = w_low.shape[1]
    tm = _tile(n, 1024)
    tn = 512
    vmem = 2 * tm * d * 4 + tm * d * 2 + 2 * d * tn * 4 + d * tn * 2 + 2 * tm * tn * 4 + 2 * tm * qk * 4 + 3 * tm * d * 4
    return pl.pallas_call(
        functools.partial(_in_kernel, tm=tm, seq_len=seq_len),
        grid=(n // tm, n_main // tn),
        in_specs=[pl.BlockSpec((tm, d), lambda i, j: (i, 0)),
                  _table_spec(n_seq, d, tm, seq_len),
                  _table_spec(n_seq, d, tm, seq_len),
                  pl.BlockSpec((1, d), lambda i, j: (0, 0)),
                  pl.BlockSpec((d, tn), lambda i, j: (0, j)),
                  pl.BlockSpec((d, rank), lambda i, j: (0, 0)),
                  pl.BlockSpec((rank, qk), lambda i, j: (0, 0)),
                  pl.BlockSpec((1, qk), lambda i, j: (0, 0))],
        out_specs=[pl.BlockSpec((tm, tn), lambda i, j: (i, j)),
                   pl.BlockSpec((tm, qk), lambda i, j: (i, 0))],
        out_shape=[jax.ShapeDtypeStruct((n, n_main), F32), jax.ShapeDtypeStruct((n, qk), F32)],
        scratch_shapes=[pltpu.VMEM((tm, d), BF16)],
        compiler_params=_params(("arbitrary", "arbitrary"), vmem),
        name="in_proj",
    )(x2, sh, sc, g.reshape(1, d), w_in, w_low, w_alpha, b_alpha.reshape(1, qk))


CONV_HALO = 32


def _ln_silu(y, g, b):
    mu = jnp.mean(y, axis=-1, keepdims=True)
    yc = y - mu
    var = jnp.mean(yc * yc, axis=-1, keepdims=True)
    yn = yc * lax.rsqrt(var + EPS) * g + b
    return yn * _sigmoid(yn)


def _conv_seq_kernel(ua_ref, ub_ref, w_ref, b_ref, lg_ref, lb_ref, y_ref, nb_ref, buf, *, tt, kw):
    t = pl.program_id(1)
    cc = buf.shape[1]

    @pl.when(t == 0)
    def _():
        buf[0:CONV_HALO, :] = jnp.zeros((CONV_HALO, cc), F32)

    buf[CONV_HALO:CONV_HALO + tt, :] = ua_ref[...] * _sigmoid(ub_ref[...])
    off = CONV_HALO - (kw - 1)
    acc = jnp.broadcast_to(b_ref[...], (tt, cc))
    for j in range(kw):
        acc = acc + w_ref[j:j + 1, :] * buf[off + j:off + j + tt, :]
    y_ref[...] = _ln_silu(acc, lg_ref[...], lb_ref[...]).astype(y_ref.dtype)

    @pl.when(t == pl.num_programs(1) - 1)
    def _():
        nb_ref[...] = buf[CONV_HALO + tt - (kw - 1):CONV_HALO + tt, :]

    buf[0:CONV_HALO, :] = buf[tt:tt + CONV_HALO, :]


def _conv_seq(u2, n_seq, seq_len, w_dw, b_dw, ln_g, ln_b):
    kw, cc = w_dw.shape
    tt = min(256, seq_len)
    nt = seq_len // tt
    row = lambda v: v.reshape(1, cc)
    vmem = 4 * tt * cc * 4 + 2 * tt * cc * 2 + (CONV_HALO + tt) * cc * 4 + 6 * tt * cc * 4
    return pl.pallas_call(
        functools.partial(_conv_seq_kernel, tt=tt, kw=kw),
        grid=(n_seq, nt),
        in_specs=[pl.BlockSpec((tt, cc), lambda b, t: (b * nt + t, 0)),
                  pl.BlockSpec((tt, cc), lambda b, t: (b * nt + t, 1)),
                  pl.BlockSpec((kw, cc), lambda b, t: (0, 0)),
                  pl.BlockSpec((1, cc), lambda b, t: (0, 0)),
                  pl.BlockSpec((1, cc), lambda b, t: (0, 0)),
                  pl.BlockSpec((1, cc), lambda b, t: (0, 0))],
        out_specs=[pl.BlockSpec((tt, cc), lambda b, t: (b * nt + t, 0)),
                   pl.BlockSpec((None, kw - 1, cc), lambda b, t: (b, 0, 0))],
        out_shape=[jax.ShapeDtypeStruct((n_seq * seq_len, cc), BF16),
                   jax.ShapeDtypeStruct((n_seq, kw - 1, cc), F32)],
        scratch_shapes=[pltpu.VMEM((CONV_HALO + tt, cc), F32)],
        compiler_params=_params(("arbitrary", "arbitrary"), vmem),
        name="conv_seq",
    )(u2, u2, w_dw, row(b_dw), row(ln_g), row(ln_b))


def _conv_step_kernel(ua_ref, ub_ref, st_ref, w_ref, b_ref, lg_ref, lb_ref, y_ref, nb_ref, fbuf, *, kw):
    t_len = ua_ref.shape[1]
    fbuf[:, 0:kw - 1, :] = st_ref[...]
    fbuf[:, kw - 1:kw - 1 + t_len, :] = ua_ref[...] * _sigmoid(ub_ref[...])
    acc = jnp.broadcast_to(b_ref[...][None], ua_ref.shape)
    for j in range(kw):
        acc = acc + w_ref[j:j + 1, :][None] * fbuf[:, j:j + t_len, :]
    y_ref[...] = _ln_silu(acc, lg_ref[...][None], lb_ref[...][None])
    nb_ref[...] = fbuf[:, t_len:t_len + kw - 1, :]


def _conv_step(u3, state, w_dw, b_dw, ln_g, ln_b):
    n_seq, t_len, _ = u3.shape
    kw, cc = w_dw.shape
    bs = min(16, n_seq)
    rows = -(-(kw - 1 + t_len) // 8) * 8
    row = lambda v: v.reshape(1, cc)
    vmem = bs * cc * 4 * (4 * t_len + 4 * 32 + rows + 6 * t_len)
    return pl.pallas_call(
        functools.partial(_conv_step_kernel, kw=kw),
        grid=(n_seq // bs,),
        in_specs=[pl.BlockSpec((bs, t_len, cc), lambda i: (i, 0, 0)),
                  pl.BlockSpec((bs, t_len, cc), lambda i: (i, 0, 1)),
                  pl.BlockSpec((bs, kw - 1, cc), lambda i: (i, 0, 0)),
                  pl.BlockSpec((kw, cc), lambda i: (0, 0)),
                  pl.BlockSpec((1, cc), lambda i: (0, 0)),
                  pl.BlockSpec((1, cc), lambda i: (0, 0)),
                  pl.BlockSpec((1, cc), lambda i: (0, 0))],
        out_specs=[pl.BlockSpec((bs, t_len, cc), lambda i: (i, 0, 0)),
                   pl.BlockSpec((bs, kw - 1, cc), lambda i: (i, 0, 0))],
        out_shape=[jax.ShapeDtypeStruct((n_seq, t_len, cc), F32),
                   jax.ShapeDtypeStruct((n_seq, kw - 1, cc), F32)],
        scratch_shapes=[pltpu.VMEM((bs, rows, cc), F32)],
        compiler_params=_params(("arbitrary",), vmem),
        name="conv_step",
    )(u3, u3, state, w_dw, row(b_dw), row(ln_g), row(ln_b))


def _gla_consts(c):
    n_lev = int(np.log2(c))
    i = np.arange(c)
    mall = [(i[None, :] <= i[:, None])]
    masks = []
    for lev in range(n_lev):
        s = 1 << lev
        m = (i // (2 * s)) * (2 * s) + s - 1
        mall.append(i[None, :] <= m[:, None])
        same = (i[:, None] // (2 * s)) == (i[None, :] // (2 * s))
        upper = ((i[:, None] // s) % 2) == 1
        lower = ((i[None, :] // s) % 2) == 0
        masks.append(same & upper & lower)
    mall = jnp.asarray(np.concatenate(mall, axis=0).astype(np.float32), dtype=BF16)
    masks = jnp.asarray(np.stack(masks).astype(np.float32))
    return mall, masks, n_lev


def _gla_head(q, k, v, la, s0, mall, mask_ref, c, n_lev):
    dk = q.shape[1]
    dv = v.shape[1]
    ball = _dot_sel(mall, la)
    bc = ball[0:c]
    vb = v.astype(BF16)
    o = _dot((q * jnp.exp(bc)).astype(BF16), s0.astype(BF16))
    o = o + jnp.sum(q * k, axis=-1, keepdims=True) * v
    att = jnp.zeros((c, c), F32)
    for lev in range(n_lev):
        bm = ball[(1 + lev) * c:(2 + lev) * c]
        ql = (q * jnp.exp(jnp.minimum(bc - bm, 0.0))).astype(BF16)
        kl = (k * jnp.exp(jnp.minimum(bm - bc, 0.0))).astype(BF16)
        a = lax.dot_general(ql, kl, (((1,), (1,)), ((), ())), preferred_element_type=F32)
        att = att + jnp.where(mask_ref[lev] > 0.5, a, 0.0)
    o = o + _dot(att.astype(BF16), vb)
    b_last = bc[c - 1:c, :]
    kd = (k * jnp.exp(b_last - bc)).astype(BF16)
    upd = lax.dot_general(kd, vb, (((0,), (0,)), ((), ())), preferred_element_type=F32)
    dec = jnp.exp(jnp.transpose(jnp.broadcast_to(b_last, (dk, dk))))
    s1 = s0 * jnp.tile(dec, (1, dv // dk)) + upd
    return o, s1


def _gla_finish(o, g, gn):
    return _rms(o, gn) * (g * _sigmoid(g))


def _gla_seq_kernel(q_ref, k_ref, v_ref, g_ref, la_ref, gn_ref, mall_ref, mask_ref, y_ref, sf_ref, s_scr,
                    *, c, n_lev, heads):
    t = pl.program_id(1)
    dk = q_ref.shape[1] // heads
    dv = v_ref.shape[1] // heads

    @pl.when(t == 0)
    def _():
        s_scr[...] = jnp.zeros(s_scr.shape, F32)

    mall = mall_ref[...]
    for h in range(heads):
        ks = slice(h * dk, (h + 1) * dk)
        vs = slice(h * dv, (h + 1) * dv)
        o, s1 = _gla_head(q_ref[:, ks] * (dk ** -0.5), k_ref[:, ks], v_ref[:, vs], la_ref[:, ks], s_scr[h],
                          mall, mask_ref, c, n_lev)
        s_scr[h] = s1
        y_ref[:, vs] = _gla_finish(o, g_ref[:, vs], gn_ref[...]).astype(y_ref.dtype)

    @pl.when(t == pl.num_programs(1) - 1)
    def _():
        sf_ref[...] = s_scr[...]


def _gla_seq(u2, la, n_seq, seq_len, gn_g, cc, qk):
    heads = GLA_HEADS
    dk = qk // heads
    dv = cc // heads
    c = int(np.gcd(seq_len, GLA_CHUNK))
    nt = seq_len // c
    mall, masks, n_lev = _gla_consts(c)
    qb = 2 * cc // qk
    vb = (2 * cc + 2 * qk) // cc
    vmem = 2 * c * (3 * qk + 3 * cc) * 4 + 3 * heads * dk * dv * 4 + 64 * c * max(dv, 128) * 4
    return pl.pallas_call(
        functools.partial(_gla_seq_kernel, c=c, n_lev=n_lev, heads=heads),
        grid=(n_seq, nt),
        in_specs=[pl.BlockSpec((c, qk), lambda b, t: (b * nt + t, qb)),
                  pl.BlockSpec((c, qk), lambda b, t: (b * nt + t, qb + 1)),
                  pl.BlockSpec((c, cc), lambda b, t: (b * nt + t, vb)),
                  pl.BlockSpec((c, cc), lambda b, t: (b * nt + t, vb + 1)),
                  pl.BlockSpec((c, qk), lambda b, t: (b * nt + t, 0)),
                  pl.BlockSpec((1, dv), lambda b, t: (0, 0)),
                  pl.BlockSpec(mall.shape, lambda b, t: (0, 0)),
                  pl.BlockSpec(masks.shape, lambda b, t: (0, 0, 0))],
        out_specs=[pl.BlockSpec((c, cc), lambda b, t: (b * nt + t, 0)),
                   pl.BlockSpec((None, heads, dk, dv), lambda b, t: (b, 0, 0, 0))],
        out_shape=[jax.ShapeDtypeStruct((n_seq * seq_len, cc), BF16),
                   jax.ShapeDtypeStruct((n_seq, heads, dk, dv), F32)],
        scratch_shapes=[pltpu.VMEM((heads, dk, dv), F32)],
        compiler_params=_params(("arbitrary", "arbitrary"), vmem),
        name="gla_seq",
    )(u2, u2, u2, u2, la, gn_g.reshape(1, dv), mall, masks)


def _gla_step_kernel(q_ref, k_ref, v_ref, g_ref, la_ref, st_ref, gn_ref, mall_ref, mask_ref, y_ref, sn_ref,
                     *, c, n_lev, heads):
    dk = q_ref.shape[2] // heads
    dv = v_ref.shape[2] // heads
    mall = mall_ref[...]

    def body(s, carry):
        for h in range(heads):
            ks = slice(h * dk, (h + 1) * dk)
            vs = slice(h * dv, (h + 1) * dv)
            o, s1 = _gla_head(q_ref[s, :, ks] * (dk ** -0.5), k_ref[s, :, ks], v_ref[s, :, vs],
                              la_ref[s, :, ks], st_ref[s, h], mall, mask_ref, c, n_lev)
            sn_ref[s, h] = s1
            y_ref[s, :, vs] = _gla_finish(o, g_ref[s, :, vs], gn_ref[...])
        return carry

    lax.fori_loop(0, q_ref.shape[0], body, 0)


def _gla_step(u3, la3, state, gn_g, cc, qk):
    n_seq, t_len, _ = u3.shape
    heads = GLA_HEADS
    dk = qk // heads
    dv = cc // heads
    assert t_len <= GLA_CHUNK and GLA_CHUNK % t_len == 0, "single-chunk path"
    mall, masks, n_lev = _gla_consts(t_len)
    bs = min(8, n_seq)
    qb = 2 * cc // qk
    vb = (2 * cc + 2 * qk) // cc
    vmem = 2 * bs * t_len * (3 * qk + 3 * cc) * 4 + 4 * bs * heads * dk * dv * 4 + 64 * 8 * max(dv, 128) * 4
    return pl.pallas_call(
        functools.partial(_gla_step_kernel, c=t_len, n_lev=n_lev, heads=heads),
        grid=(n_seq // bs,),
        in_specs=[pl.BlockSpec((bs, t_len, qk), lambda i: (i, 0, qb)),
                  pl.BlockSpec((bs, t_len, qk), lambda i: (i, 0, qb + 1)),
                  pl.BlockSpec((bs, t_len, cc), lambda i: (i, 0, vb)),
                  pl.BlockSpec((bs, t_len, cc), lambda i: (i, 0, vb + 1)),
                  pl.BlockSpec((bs, t_len, qk), lambda i: (i, 0, 0)),
                  pl.BlockSpec((bs, heads, dk, dv), lambda i: (i, 0, 0, 0)),
                  pl.BlockSpec((1, dv), lambda i: (0, 0)),
                  pl.BlockSpec(mall.shape, lambda i: (0, 0)),
                  pl.BlockSpec(masks.shape, lambda i: (0, 0, 0))],
        out_specs=[pl.BlockSpec((bs, t_len, cc), lambda i: (i, 0, 0)),
                   pl.BlockSpec((bs, heads, dk, dv), lambda i: (i, 0, 0, 0))],
        out_shape=[jax.ShapeDtypeStruct((n_seq, t_len, cc), F32),
                   jax.ShapeDtypeStruct((n_seq, heads, dk, dv), F32)],
        compiler_params=_params(("arbitrary",), vmem),
        name="gla_step",
    )(u3, u3, u3, u3, la3, state, gn_g.reshape(1, dv), mall, masks)


def _out_kernel(x_ref, yc_ref, yg_ref, wo_ref, g1_ref, sh_ref, sc_ref, n2_ref, wr_ref, br_ref,
                x1_ref, hp_ref, ti_ref, tw_ref, *, tm, seq_len):
    i = pl.program_id(0)
    cc = yc_ref.shape[1]
    d = x_ref.shape[1]
    n_exp = wr_ref.shape[1]
    mix = _dot(yc_ref[...].astype(BF16), wo_ref[0:cc, :]) + _dot(yg_ref[...].astype(BF16), wo_ref[cc:, :])
    x1 = x_ref[...] + _expand_rows(g1_ref, i, tm, seq_len) * mix
    x1_ref[...] = x1
    h2 = _rms(x1, n2_ref[...]) * (1.0 + _expand_rows(sc_ref, i, tm, seq_len)) + _expand_rows(sh_ref, i, tm, seq_len)
    hb = h2.astype(BF16)
    hbits = lax.bitcast_convert_type(hb.astype(F32), I32)
    hp_ref[...] = lax.shift_right_logical(hbits[:, :d // 2], 16) | (hbits[:, d // 2:] & HI16)

    logits = _dot(hb, wr_ref[...].astype(BF16)) + br_ref[...]
    lane = lax.broadcasted_iota(I32, logits.shape, 1)
    vals = []
    for r in range(TOP_K):
        m = jnp.max(logits, axis=-1, keepdims=True)
        idx = jnp.min(jnp.where(logits == m, lane, n_exp), axis=-1, keepdims=True)
        vals.append(m)
        ti_ref[:, r:r + 1] = idx
        logits = jnp.where(lane == idx, -jnp.inf, logits)
    ex = [jnp.exp(v - vals[0]) for v in vals]
    den = ex[0] + ex[1] + ex[2] + ex[3]
    for r in range(TOP_K):
        tw_ref[:, r:r + 1] = ex[r] / den


def _out_proj(x2, yc, yg, wo_bf, g1, sh2, sc2, n2, w_router, b_router, seq_len):
    n, d = x2.shape
    cc = yc.shape[1]
    n_seq = g1.shape[0]
    n_exp = w_router.shape[1]
    tm = _tile(n, 512)
    vmem = (4 * tm * d * 4 + 2 * tm * d * 2 + 4 * tm * cc * 4 + 2 * 2 * cc * d * 2 + 8 * tm * d * 4)
    tok = lambda i: (i, 0)
    fix = lambda i: (0, 0)
    return pl.pallas_call(
        functools.partial(_out_kernel, tm=tm, seq_len=seq_len),
        grid=(n // tm,),
        in_specs=[pl.BlockSpec((tm, d), tok),
                  pl.BlockSpec((tm, cc), tok),
                  pl.BlockSpec((tm, cc), tok),
                  pl.BlockSpec((2 * cc, d), fix),
                  _table_spec(n_seq, d, tm, seq_len),
                  _table_spec(n_seq, d, tm, seq_len),
                  _table_spec(n_seq, d, tm, seq_len),
                  pl.BlockSpec((1, d), fix),
                  pl.BlockSpec((d, n_exp), fix),
                  pl.BlockSpec((1, n_exp), fix)],
        out_specs=[pl.BlockSpec((tm, d), tok),
                   pl.BlockSpec((tm, d // 2), tok),
                   pl.BlockSpec((tm, TOP_K), tok),
                   pl.BlockSpec((tm, TOP_K), tok)],
        out_shape=[jax.ShapeDtypeStruct((n, d), F32),
                   jax.ShapeDtypeStruct((n, d // 2), I32),
                   jax.ShapeDtypeStruct((n, TOP_K), I32),
                   jax.ShapeDtypeStruct((n, TOP_K), F32)],
        compiler_params=_params(("arbitrary",), vmem),
        name="out_proj",
    )(x2, yc, yg, wo_bf, g1, sh2, sc2, n2.reshape(1, d), w_router, b_router.reshape(1, n_exp))


def _rank_kernel(ti_ref, slot_ref, cnt_ref, off_ref, carry, offs, *, tr, n_exp, group):
    p = pl.program_id(0)
    i = pl.program_id(1)
    lane = lax.broadcasted_iota(I32, (tr, n_exp), 1)
    ti = ti_ref[...]
    hot = jnp.zeros((tr, n_exp), F32)
    for k in range(TOP_K):
        hot = hot + (lane == ti[:, k:k + 1]).astype(F32)

    @pl.when((p == 0) & (i == 0))
    def _():
        carry[...] = jnp.zeros(carry.shape, F32)

    @pl.when((p == 1) & (i == 0))
    def _():
        cnt = carry[0:1, :]
        padded = jnp.ceil(cnt * (1.0 / group)) * group
        r = lax.broadcasted_iota(I32, (n_exp, n_exp), 0)
        c = lax.broadcasted_iota(I32, (n_exp, n_exp), 1)
        before = (r < c).astype(BF16)
        hi, mid, lo = _split3(jnp.broadcast_to(padded, (8, n_exp)))
        off = _dot(hi, before) + _dot(mid, before) + _dot(lo, before)
        offs[...] = off
        cnt_ref[...] = cnt.astype(I32)
        off_ref[...] = off[0:1, :].astype(I32)
        carry[...] = jnp.zeros(carry.shape, F32)

    @pl.when(p == 1)
    def _():
        r = lax.broadcasted_iota(I32, (tr, tr), 0)
        c = lax.broadcasted_iota(I32, (tr, tr), 1)
        earlier = (c < r).astype(BF16)
        tot = _dot(earlier, hot.astype(BF16)) + carry[0:1, :] + offs[0:1, :]
        for k in range(TOP_K):
            sel = jnp.where(lane == ti[:, k:k + 1], tot, 0.0)
            slot_ref[:, k:k + 1] = jnp.sum(sel, axis=-1, keepdims=True).astype(I32)

    carry[...] = carry[...] + jnp.sum(hot, axis=0, keepdims=True)


def _rank(top_i, n_exp, group):
    n = top_i.shape[0]
    tr = _tile(n, 256)
    return pl.pallas_call(
        functools.partial(_rank_kernel, tr=tr, n_exp=n_exp, group=group),
        grid=(2, n // tr),
        in_specs=[pl.BlockSpec((tr, TOP_K), lambda p, i: (i, 0))],
        out_specs=[pl.BlockSpec((tr, TOP_K), lambda p, i: (i * p, 0)),
                   pl.BlockSpec((1, n_exp), lambda p, i: (0, 0)),
                   pl.BlockSpec((1, n_exp), lambda p, i: (0, 0))],
        out_shape=[jax.ShapeDtypeStruct((n, TOP_K), I32),
                   jax.ShapeDtypeStruct((1, n_exp), I32),
                   jax.ShapeDtypeStruct((1, n_exp), I32)],
        scratch_shapes=[pltpu.VMEM((8, n_exp), F32), pltpu.VMEM((8, n_exp), F32)],
        compiler_params=_params(("arbitrary", "arbitrary"), 8 * tr * tr * 4),
        name="rank",
    )(top_i)


def _dispatch_kernel(slot_ref, h_ref, xin_ref, xs_ref, sem, *, td):
    del xin_ref

    def issue(r, carry):
        for k in range(TOP_K):
            s = slot_ref[0, r * TOP_K + k]
            pltpu.make_async_copy(h_ref.at[pl.ds(r, 1), :], xs_ref.at[pl.ds(s, 1), :], sem).start()
        return carry

    lax.fori_loop(0, td, issue, 0)
    for k in range(TOP_K):
        pltpu.make_async_copy(h_ref, xs_ref.at[pl.ds(0, td), :], sem).wait()


def _dispatch(slots, hp, xs):
    n, dw = hp.shape
    td = _tile(n, 256)
    nt = n // td
    return pl.pallas_call(
        functools.partial(_dispatch_kernel, td=td),
        grid=(nt,),
        in_specs=[pl.BlockSpec((None, 1, td * TOP_K), lambda i: (i, 0, 0), memory_space=pltpu.SMEM),
                  pl.BlockSpec((td, dw), lambda i: (i, 0)),
                  pl.BlockSpec(memory_space=pl.ANY)],
        out_specs=pl.BlockSpec(memory_space=pl.ANY),
        out_shape=jax.ShapeDtypeStruct(xs.shape, xs.dtype),
        scratch_shapes=[pltpu.SemaphoreType.DMA],
        input_output_aliases={2: 0},
        compiler_params=_params(("arbitrary",), 2 * td * dw * 4),
        name="dispatch",
    )(slots.reshape(nt, 1, td * TOP_K), hp, xs)


MOE_CHUNK = 2048
MOE_SUB = 256
MOE_TF = 256
MOE_TN = 256


def _moe_kernel(che, chrow, chn, tail, xs_ref, wg_ref, wu_ref, bg_ref, bu_ref, wd_ref, bd_ref, yb_ref,
                xbuf, xb, act, wgb, wub, wdb, obuf, semx, semo, *, n_f, n_n, n_chunks):
    c = pl.program_id(0)
    j = pl.program_id(1)
    nsb = chn[c]
    half = xbuf.shape[1]
    sub = MOE_SUB

    def x_copy(cc):
        return pltpu.make_async_copy(xs_ref.at[pl.ds(pl.multiple_of(chrow[cc], sub), MOE_CHUNK), :], xbuf, semx)

    def o_copy(slot, i, jj):
        row = pl.multiple_of(chrow[c] + i * sub, sub)
        return pltpu.make_async_copy(obuf.at[slot, pl.ds(pl.multiple_of(i * sub, sub), sub), :],
                                     yb_ref.at[pl.ds(row, sub), pl.ds(jj * MOE_TN, MOE_TN)], semo.at[slot])

    @pl.when((c == 0) & (j == 0))
    def _():
        x_copy(0).start()

    @pl.when((j == 0) & (nsb > 0))
    def _():
        x_copy(c).wait()

        def unpack(i, carry):
            rows = pl.ds(pl.multiple_of(i * sub, sub), sub)
            w = xbuf[rows, :]
            xb[rows, 0:half] = lax.bitcast_convert_type(lax.shift_left(w, 16), F32).astype(BF16)
            xb[rows, half:] = lax.bitcast_convert_type(w & HI16, F32).astype(BF16)
            return carry

        lax.fori_loop(0, nsb, unpack, 0)

    nxt = jnp.minimum(c + 1, n_chunks - 1)

    @pl.when((j == 1) & (c + 1 < n_chunks) & (chn[nxt] > 0))
    def _():
        x_copy(nxt).start()

    @pl.when((j < n_f) & (nsb > 0))
    def _():
        wgb[...] = wg_ref[...].astype(BF16)
        wub[...] = wu_ref[...].astype(BF16)

        def sub_block(i, carry):
            rows = pl.ds(pl.multiple_of(i * sub, sub), sub)
            x = xb[rows, :]
            gate = jnp.minimum(_dot(x, wgb[...]) + bg_ref[...], SWIGLU_LIMIT)
            up = jnp.clip(_dot(x, wub[...]) + bu_ref[...], -SWIGLU_LIMIT, SWIGLU_LIMIT)
            glu = gate * _sigmoid(SWIGLU_ALPHA * gate)
            act[j, rows, :] = ((up + 1.0) * glu).astype(BF16)
            return carry

        lax.fori_loop(0, nsb, sub_block, 0)

    @pl.when((j >= n_f) & (nsb > 0))
    def _():
        jj = j - n_f
        slot = jj % 2
        wdb[...] = wd_ref[...].astype(BF16)

        def retire(s):
            def one(i, carry):
                o_copy(s, 0, 0).wait()
                return carry
            lax.fori_loop(0, nsb, one, 0)

        @pl.when(jj >= 2)
        def _():
            retire(slot)

        def sub_block(i, carry):
            rows = pl.ds(pl.multiple_of(i * sub, sub), sub)
            a = jnp.concatenate([act[f, rows, :] for f in range(n_f)], axis=1)
            obuf[slot, rows, :] = _dot(a, wdb[...]) + bd_ref[...]
            for js in range(n_n):
                @pl.when(jj == js)
                def _():
                    o_copy(slot, i, js).start()
            return carry

        lax.fori_loop(0, nsb, sub_block, 0)

        @pl.when(jj == n_n - 1)
        def _():
            retire(slot)
            if n_n >= 2:
                retire(1 - slot)

    @pl.when((c == n_chunks - 1) & (j == n_f + n_n - 1))
    def _():
        n_tail = (yb_ref.shape[0] - tail[0]) // sub
        obuf[0, 0:sub, :] = jnp.zeros((sub, MOE_TN), F32)

        def z_copy(i, js):
            row = pl.multiple_of(tail[0] + i * sub, sub)
            return pltpu.make_async_copy(obuf.at[0, pl.ds(0, sub), :],
                                         yb_ref.at[pl.ds(row, sub), pl.ds(js * MOE_TN, MOE_TN)], semo.at[0])

        def fill(i, carry):
            for js in range(n_n):
                z_copy(i, js).start()
            return carry

        def drain(i, carry):
            for js in range(n_n):
                z_copy(0, js).wait()
            return carry

        lax.fori_loop(0, n_tail, fill, 0)
        lax.fori_loop(0, n_tail, drain, 0)


def _moe(xs, yb_rows, ch_e, ch_row, ch_n, tail, w_gate, b_gate, w_up, b_up, w_down, b_down):
    n_exp, d, f = w_gate.shape
    half = xs.shape[1]
    n_f = f // MOE_TF
    n_n = d // MOE_TN
    n_chunks = ch_e.shape[0]

    def w1(c, j, che, chrow, chn, tail):
        return (che[c], 0, jnp.where(chn[c] > 0, jnp.minimum(j, n_f - 1), n_f - 1))

    def w2(c, j, che, chrow, chn, tail):
        return (che[c], 0, jnp.where(chn[c] > 0, jnp.maximum(j - n_f, 0), n_n - 1))

    vmem = (MOE_CHUNK * half * 4 + MOE_CHUNK * d * 2 + MOE_CHUNK * f * 2
            + 2 * 2 * d * MOE_TF * 4 + 2 * f * MOE_TN * 4 + 2 * d * MOE_TF * 2 + f * MOE_TN * 2
            + 2 * MOE_CHUNK * MOE_TN * 4 + 8 * MOE_SUB * max(d, f) * 4)
    grid_spec = pltpu.PrefetchScalarGridSpec(
        num_scalar_prefetch=4,
        grid=(n_chunks, n_f + n_n),
        in_specs=[pl.BlockSpec(memory_space=pl.ANY),
                  pl.BlockSpec((None, d, MOE_TF), w1),
                  pl.BlockSpec((None, d, MOE_TF), w1),
                  pl.BlockSpec((None, 1, MOE_TF), w1),
                  pl.BlockSpec((None, 1, MOE_TF), w1),
                  pl.BlockSpec((None, f, MOE_TN), w2),
                  pl.BlockSpec((None, 1, MOE_TN), w2)],
        out_specs=pl.BlockSpec(memory_space=pl.ANY),
        scratch_shapes=[pltpu.VMEM((MOE_CHUNK, half), I32),
                        pltpu.VMEM((MOE_CHUNK, d), BF16),
                        pltpu.VMEM((n_f, MOE_CHUNK, MOE_TF), BF16),
                        pltpu.VMEM((d, MOE_TF), BF16),
                        pltpu.VMEM((d, MOE_TF), BF16),
                        pltpu.VMEM((f, MOE_TN), BF16),
                        pltpu.VMEM((2, MOE_CHUNK, MOE_TN), F32),
                        pltpu.SemaphoreType.DMA,
                        pltpu.SemaphoreType.DMA((2,))])
    return pl.pallas_call(
        functools.partial(_moe_kernel, n_f=n_f, n_n=n_n, n_chunks=n_chunks),
        grid_spec=grid_spec,
        out_shape=jax.ShapeDtypeStruct((yb_rows, d), F32),
        compiler_params=_params(("arbitrary", "arbitrary"), vmem),
        name="moe",
    )(ch_e, ch_row, ch_n, tail, xs, w_gate, w_up, b_gate.reshape(n_exp, 1, f), b_up.reshape(n_exp, 1, f),
      w_down, b_down.reshape(n_exp, 1, d))


def _chunk_table(cnt, off, n_chunks):
    n_exp = cnt.shape[0]
    per = MOE_CHUNK // MOE_SUB
    nsb = (cnt + MOE_SUB - 1) // MOE_SUB
    nch = (nsb + per - 1) // per
    end = jnp.cumsum(nch)
    start = end - nch
    cidx = jnp.arange(n_chunks, dtype=I32)
    e = jnp.minimum(jnp.searchsorted(end, cidx, side="right"), n_exp - 1).astype(I32)
    used = cidx < end[-1]
    last_e = jnp.minimum(jnp.searchsorted(end, end[-1] - 1, side="right"), n_exp - 1).astype(I32)
    local = cidx - start[e]
    ch_e = jnp.where(used, e, last_e)
    ch_row = jnp.where(used, off[e] + local * MOE_CHUNK, 0).astype(I32)
    ch_n = jnp.where(used, jnp.minimum(per, nsb[e] - local * per), 0).astype(I32)
    tail = (off[-1] + nsb[-1] * MOE_SUB).astype(I32).reshape(1)
    return ch_e, ch_row, ch_n, tail


def _combine_kernel(slot_ref, tw_ref, x1_ref, g2_ref, gf_ref, yb_ref, y_ref, gbuf, sem, *, tc, seq_len):
    i = pl.program_id(0)

    def issue(r, carry):
        for k in range(TOP_K):
            s = slot_ref[0, r * TOP_K + k]
            pltpu.make_async_copy(yb_ref.at[pl.ds(s, 1), :], gbuf.at[k, pl.ds(r, 1), :], sem).start()
        return carry

    lax.fori_loop(0, tc, issue, 0)
    for k in range(TOP_K):
        pltpu.make_async_copy(yb_ref.at[pl.ds(0, tc), :], gbuf.at[k], sem).wait()
    tw = tw_ref[...]
    y = tw[:, 0:1] * gbuf[0]
    for k in range(1, TOP_K):
        y = y + tw[:, k:k + 1] * gbuf[k]
    x2 = x1_ref[...] + _expand_rows(g2_ref, i, tc, seq_len) * y
    y_ref[...] = _rms(x2, gf_ref[...])


def _combine(slots, top_w, x1, g2, gf, yb, seq_len):
    n, d = x1.shape
    n_seq = g2.shape[0]
    tc = _tile(n, 128)
    nt = n // tc
    vmem = TOP_K * tc * d * 4 + 4 * tc * d * 4 + 6 * tc * d * 4
    return pl.pallas_call(
        functools.partial(_combine_kernel, tc=tc, seq_len=seq_len),
        grid=(nt,),
        in_specs=[pl.BlockSpec((None, 1, tc * TOP_K), lambda i: (i, 0, 0), memory_space=pltpu.SMEM),
                  pl.BlockSpec((tc, TOP_K), lambda i: (i, 0)),
                  pl.BlockSpec((tc, d), lambda i: (i, 0)),
                  _table_spec(n_seq, d, tc, seq_len),
                  pl.BlockSpec((1, d), lambda i: (0, 0)),
                  pl.BlockSpec(memory_space=pl.ANY)],
        out_specs=pl.BlockSpec((tc, d), lambda i: (i, 0)),
        out_shape=jax.ShapeDtypeStruct((n, d), F32),
        scratch_shapes=[pltpu.VMEM((TOP_K, tc, d), F32), pltpu.SemaphoreType.DMA],
        compiler_params=_params(("arbitrary",), vmem),
        name="combine",
    )(slots.reshape(nt, 1, tc * TOP_K), top_w, x1, g2, gf.reshape(1, d), yb)


def _layer(xp, xs, cp, cs, conv_state, gla_state, w):
    (w_ada, b_ada, norm1_g, w_in, w_dw, b_dw, conv_ln_g, conv_ln_b, w_alpha, b_alpha, gla_norm_g, w_out,
     norm2_g, w_router, b_router, w_gate, b_gate, w_up, b_up, w_down, b_down, final_g) = w
    bp, tp, d = xp.shape
    bs, ts, _ = xs.shape
    cc = w_dw.shape[1]
    qk = w_alpha.shape[1]
    n_exp = w_router.shape[1]
    n_main = 2 * cc + 2 * qk + 2 * cc
    np_, ns_ = bp * tp, bs * ts

    mod = _ada(jnp.concatenate([cp, cs], axis=0), w_ada, b_ada)
    mods_p = [mod[:bp, i * d:(i + 1) * d] for i in range(6)]
    mods_s = [mod[bp:, i * d:(i + 1) * d] for i in range(6)]
    w_low = w_in[:, n_main:]
    wo_bf = w_out.astype(BF16)

    xp2 = xp.reshape(np_, d)
    xs2 = xs.reshape(ns_, d)

    u_p, la_p = _in_proj(xp2, mods_p[0], mods_p[1], norm1_g, w_in, w_low, w_alpha, b_alpha, tp, n_main)
    yc_p, conv_p = _conv_seq(u_p, bp, tp, w_dw, b_dw, conv_ln_g, conv_ln_b)
    yg_p, gla_p = _gla_seq(u_p, la_p, bp, tp, gla_norm_g, cc, qk)
    x1_p, hp_p, ti_p, tw_p = _out_proj(xp2, yc_p, yg_p, wo_bf, mods_p[2], mods_p[3], mods_p[4], norm2_g,
                                       w_router, b_router, tp)

    u_s, la_s = _in_proj(xs2, mods_s[0], mods_s[1], norm1_g, w_in, w_low, w_alpha, b_alpha, ts, n_main)
    yc_s, conv_s = _conv_step(u_s.reshape(bs, ts, n_main), conv_state, w_dw, b_dw, conv_ln_g, conv_ln_b)
    yg_s, gla_s = _gla_step(u_s.reshape(bs, ts, n_main), la_s.reshape(bs, ts, qk), gla_state, gla_norm_g, cc, qk)
    x1_s, hp_s, ti_s, tw_s = _out_proj(xs2, yc_s.reshape(ns_, cc), yg_s.reshape(ns_, cc), wo_bf, mods_s[2],
                                       mods_s[3], mods_s[4], norm2_g, w_router, b_router, ts)

    n_asg = (np_ + ns_) * TOP_K
    slots, cnt, off = _rank(jnp.concatenate([ti_p, ti_s], axis=0), n_exp, MOE_SUB)
    rows = -(-(n_asg + n_exp * MOE_SUB) // MOE_SUB) * MOE_SUB
    xsort = jnp.zeros((rows + MOE_CHUNK, d // 2), I32)
    xsort = _dispatch(slots[:np_], hp_p, xsort)
    xsort = _dispatch(slots[np_:], hp_s, xsort)
    n_chunks = (n_asg + n_exp * MOE_SUB) // MOE_CHUNK + n_exp
    ch_e, ch_row, ch_n, tail = _chunk_table(cnt[0], off[0], n_chunks)
    yb = _moe(xsort, rows, ch_e, ch_row, ch_n, tail, w_gate, b_gate, w_up, b_up, w_down, b_down)

    y_p = _combine(slots[:np_], tw_p, x1_p, mods_p[5], final_g, yb, tp)
    y_s = _combine(slots[np_:], tw_s, x1_s, mods_s[5], final_g, yb, ts)
    return y_p.reshape(bp, tp, d), y_s.reshape(bs, ts, d), conv_p, gla_p, conv_s, gla_s


def kernel(x_prompt, x_sample, c_prompt, c_sample, state_conv, state_gla, w_ada, b_ada, norm1_g, w_in, w_dw, b_dw,
           conv_ln_g, conv_ln_b, w_alpha, b_alpha, gla_norm_g, w_out, norm2_g, w_router, b_router, w_gate, b_gate,
           w_up, b_up, w_down, b_down, final_norm_g):
    assert w_ada.shape[0] == 1, "the final norm is fused into the (single) layer"
    first = lambda a: a.reshape(a.shape[1:])
    w = tuple(first(a) for a in (w_ada, b_ada, norm1_g, w_in, w_dw, b_dw, conv_ln_g, conv_ln_b, w_alpha, b_alpha,
                                 gla_norm_g, w_out, norm2_g, w_router, b_router, w_gate, b_gate, w_up, b_up,
                                 w_down, b_down)) + (final_norm_g,)
    y_p, y_s, conv_p, gla_p, conv_s, gla_s = _layer(x_prompt, x_sample, c_prompt, c_sample, first(state_conv),
                                                    first(state_gla), w)
    return (y_p, y_s, conv_p[None], gla_p[None].astype(x_prompt.dtype), conv_s[None].astype(state_conv.dtype),
            gla_s[None].astype(state_gla.dtype))
```

```python
import functools

import numpy as np
import jax
import jax.numpy as jnp
from jax import lax
from jax.experimental import pallas as pl
from jax.experimental.pallas import tpu as pltpu

F32 = jnp.float32
BF16 = jnp.bfloat16
I32 = jnp.int32

EPS = 1e-5
GLA_HEADS = 4
GLA_GATE_NORM = 16.0
GLA_CHUNK = 64
TOP_K = 4
SWIGLU_LIMIT = 7.0
SWIGLU_ALPHA = 1.702

V7X_LANES = 128
V7X_VMEM_BYTES = 64 * 1024 * 1024
VMEM_CAP_BYTES = V7X_VMEM_BYTES - 6 * 1024 * 1024


def _params(semantics, vmem_bytes):
    limit = int(min(VMEM_CAP_BYTES, max(vmem_bytes * 5 // 4 + (4 << 20), 16 << 20)))
    return pltpu.CompilerParams(dimension_semantics=semantics, vmem_limit_bytes=limit)


def _tile(n, want):
    t = min(want, n)
    while n % t or t % 8:
        t -= 8 if t % 8 == 0 else t % 8
        assert t > 0, (n, want)
    return t


NT_DIMS = (((1,), (1,)), ((), ()))


def _dot(a, b):
    return jnp.dot(a, b, preferred_element_type=F32)


def _split3(v):
    hi = v.astype(BF16)
    r = v - hi.astype(F32)
    mid = r.astype(BF16)
    lo = (r - mid.astype(F32)).astype(BF16)
    return hi, mid, lo


def _dot_sel(m01, v):
    hi, mid, lo = _split3(v)
    return _dot(m01, hi) + _dot(m01, mid) + _dot(m01, lo)


def _sigmoid(x):
    return jax.nn.sigmoid(x)


def _rms(x, g):
    return x * lax.rsqrt(jnp.mean(x * x, axis=-1, keepdims=True) + EPS) * g


def _expand_rows(m_ref, i, tm, seq_len):
    if seq_len >= tm:
        return m_ref[pl.ds((i * tm) // seq_len, 1), :]
    nseq = tm // seq_len
    r = lax.broadcasted_iota(I32, (tm, nseq), 0)
    c = lax.broadcasted_iota(I32, (tm, nseq), 1) * seq_len
    sel = ((r >= c) & (r < c + seq_len)).astype(BF16)
    return _dot_sel(sel, m_ref[...])


def _table_spec(n_seq, d, tm, seq_len):
    if seq_len >= tm:
        return pl.BlockSpec((n_seq, d), lambda i, *_: (0, 0))
    return pl.BlockSpec((tm // seq_len, d), lambda i, *_: (i, 0))


def _ada_kernel(c_ref, w_ref, b_ref, o_ref):
    c = c_ref[...]
    s = (c * _sigmoid(c)).astype(BF16)
    o_ref[...] = _dot(s, w_ref[...].astype(BF16)) + b_ref[...]


def _ada(c_all, w_ada, b_ada):
    bc, d = c_all.shape
    n6 = w_ada.shape[1]
    tn = min(1024, n6)
    return pl.pallas_call(
        _ada_kernel,
        grid=(n6 // tn,),
        in_specs=[pl.BlockSpec((bc, d), lambda j: (0, 0)),
                  pl.BlockSpec((d, tn), lambda j: (0, j)),
                  pl.BlockSpec((1, tn), lambda j: (0, j))],
        out_specs=pl.BlockSpec((bc, tn), lambda j: (0, j)),
        out_shape=jax.ShapeDtypeStruct((bc, n6), F32),
        compiler_params=_params(("arbitrary",), 2 * d * tn * 4 + d * tn * 2 + 4 * bc * (d + tn) * 4),
        name="ada",
    )(c_all, w_ada, b_ada.reshape(1, n6))


IN_TN = 512


def _in_kernel(x_ref, sh_ref, sc_ref, g_ref, w_hbm, wlow_ref, wal_ref, bal_ref, u_ref, la_ref, w_vmem, sem,
               *, tm, seq_len):
    i = pl.program_id(0)
    load_w = pltpu.make_async_copy(w_hbm, w_vmem, sem)

    @pl.when(i == 0)
    def _():
        load_w.start()

    xn = _rms(x_ref[...], g_ref[...])
    h = xn * (1.0 + _expand_rows(sc_ref, i, tm, seq_len)) + _expand_rows(sh_ref, i, tm, seq_len)
    hb = h.astype(BF16)
    a_low = lax.dot_general(hb, wlow_ref[...].astype(BF16), NT_DIMS, preferred_element_type=F32)
    z = _dot(a_low.astype(BF16), wal_ref[...].astype(BF16)) + bal_ref[...]
    la_ref[...] = (jnp.minimum(z, 0.0) - jnp.log(1.0 + jnp.exp(-jnp.abs(z)))) * (1.0 / GLA_GATE_NORM)

    @pl.when(i == 0)
    def _():
        load_w.wait()

    for jn in range(u_ref.shape[1] // IN_TN):
        cols = slice(jn * IN_TN, (jn + 1) * IN_TN)
        u_ref[:, cols] = lax.dot_general(hb, w_vmem[cols, :], NT_DIMS, preferred_element_type=F32)


def _in_proj(x2, sh, sc, g, w_main_bf, w_low, w_alpha, b_alpha, seq_len):
    n, d = x2.shape
    n_main = w_main_bf.shape[0]
    n_seq = sh.shape[0]
    qk = w_alpha.shape[1]
    rank = w_low.shape[0]
    tm = _tile(n, 256)
    vmem = d * n_main * 2 + 2 * tm * d * 4 + 2 * tm * n_main * 4 + 2 * tm * qk * 4 + 4 * tm * d * 4
    fix = lambda i: (0, 0)
    return pl.pallas_call(
        functools.partial(_in_kernel, tm=tm, seq_len=seq_len),
        grid=(n // tm,),
        in_specs=[pl.BlockSpec((tm, d), lambda i: (i, 0)),
                  _table_spec(n_seq, d, tm, seq_len),
                  _table_spec(n_seq, d, tm, seq_len),
                  pl.BlockSpec((1, d), fix),
                  pl.BlockSpec(memory_space=pl.ANY),
                  pl.BlockSpec((rank, d), fix),
                  pl.BlockSpec((rank, qk), fix),
                  pl.BlockSpec((1, qk), fix)],
        out_specs=[pl.BlockSpec((tm, n_main), lambda i: (i, 0)),
                   pl.BlockSpec((tm, qk), lambda i: (i, 0))],
        out_shape=[jax.ShapeDtypeStruct((n, n_main), F32), jax.ShapeDtypeStruct((n, qk), F32)],
        scratch_shapes=[pltpu.VMEM((n_main, d), BF16), pltpu.SemaphoreType.DMA],
        compiler_params=_params(("arbitrary",), vmem),
        name="in_proj",
    )(x2, sh, sc, g.reshape(1, d), w_main_bf, w_low, w_alpha, b_alpha.reshape(1, qk))


CONV_HALO = 32
CONV_ROWS = 16


def _ln_silu(y, g, b):
    mu = jnp.mean(y, axis=-1, keepdims=True)
    yc = y - mu
    var = jnp.mean(yc * yc, axis=-1, keepdims=True)
    yn = yc * lax.rsqrt(var + EPS) * g + b
    return yn * _sigmoid(yn)


def _conv_seq_kernel(ua_ref, ub_ref, w_ref, b_ref, lg_ref, lb_ref, y_ref, nb_ref, buf, shifted, *, tt, kw):
    t = pl.program_id(1)
    cc = buf.shape[1]

    @pl.when(t == 0)
    def _():
        buf[0:CONV_HALO, :] = jnp.zeros((CONV_HALO, cc), F32)

    buf[CONV_HALO:CONV_HALO + tt, :] = ua_ref[...] * _sigmoid(ub_ref[...])
    span = shifted.shape[1]
    for r in range(1, 8):
        shifted[r - 1] = buf[r:r + span, :]
    first = CONV_HALO - (kw - 1)

    def block(rb, carry):
        base = rb * CONV_ROWS
        acc = jnp.broadcast_to(b_ref[...], (CONV_ROWS, cc))
        for j in range(kw):
            q, r = divmod(first + j, 8)
            rows = pl.ds(pl.multiple_of(base + 8 * q, 8), CONV_ROWS)
            acc = acc + w_ref[j:j + 1, :] * (buf[rows, :] if r == 0 else shifted[r - 1, rows, :])
        y_ref[pl.ds(pl.multiple_of(base, CONV_ROWS), CONV_ROWS), :] = _ln_silu(
            acc, lg_ref[...], lb_ref[...]).astype(y_ref.dtype)
        return carry

    lax.fori_loop(0, tt // CONV_ROWS, block, 0)

    @pl.when(t == pl.num_programs(1) - 1)
    def _():
        nb_ref[...] = buf[CONV_HALO + tt - (kw - 1):CONV_HALO + tt, :]

    buf[0:CONV_HALO, :] = buf[tt:tt + CONV_HALO, :]


def _conv_seq(u2, n_seq, seq_len, w_dw, b_dw, ln_g, ln_b):
    kw, cc = w_dw.shape
    assert kw - 1 <= CONV_HALO
    tt = _tile(seq_len, 256)
    assert tt % CONV_ROWS == 0 and tt >= CONV_HALO
    nt = seq_len // tt
    span = tt + CONV_HALO - 8
    row = lambda v: v.reshape(1, cc)
    vmem = 4 * tt * cc * 4 + 2 * tt * cc * 2 + (CONV_HALO + tt + 7 * span) * cc * 4 + 4 * tt * cc * 4
    return pl.pallas_call(
        functools.partial(_conv_seq_kernel, tt=tt, kw=kw),
        grid=(n_seq, nt),
        in_specs=[pl.BlockSpec((tt, cc), lambda b, t: (b * nt + t, 0)),
                  pl.BlockSpec((tt, cc), lambda b, t: (b * nt + t, 1)),
                  pl.BlockSpec((kw, cc), lambda b, t: (0, 0)),
                  pl.BlockSpec((1, cc), lambda b, t: (0, 0)),
                  pl.BlockSpec((1, cc), lambda b, t: (0, 0)),
                  pl.BlockSpec((1, cc), lambda b, t: (0, 0))],
        out_specs=[pl.BlockSpec((tt, cc), lambda b, t: (b * nt + t, 0)),
                   pl.BlockSpec((None, None, kw - 1, cc), lambda b, t: (0, b, 0, 0))],
        out_shape=[jax.ShapeDtypeStruct((n_seq * seq_len, cc), BF16),
                   jax.ShapeDtypeStruct((1, n_seq, kw - 1, cc), F32)],
        scratch_shapes=[pltpu.VMEM((CONV_HALO + tt, cc), F32), pltpu.VMEM((7, span, cc), F32)],
        compiler_params=_params(("arbitrary", "arbitrary"), vmem),
        name="conv_seq",
    )(u2, u2, w_dw, row(b_dw), row(ln_g), row(ln_b))


def _conv_step_kernel(ua_ref, ub_ref, st_ref, w_ref, b_ref, lg_ref, lb_ref, y_ref, nb_ref, gbuf, ybuf,
                      *, kw, t_len):
    hist = kw - 1
    n_seq, cc = st_ref.shape[1], st_ref.shape[2]
    n_lt = gbuf.shape[0]
    lanes = gbuf.shape[2]
    glu = ua_ref[...] * _sigmoid(ub_ref[...])
    for lt in range(n_lt):
        gbuf[lt] = glu[:, lt * lanes:(lt + 1) * lanes]

    def slab(r):
        if r < hist:
            return st_ref[r]
        step = pl.ds(r - hist, n_seq, stride=t_len)
        return jnp.concatenate([gbuf[lt, step, :] for lt in range(n_lt)], axis=1)

    for t in range(t_len):
        acc = jnp.broadcast_to(b_ref[...], (n_seq, cc))
        for j in range(kw):
            acc = acc + w_ref[j:j + 1, :] * slab(t + j)
        y = _ln_silu(acc, lg_ref[...], lb_ref[...])
        for lt in range(n_lt):
            ybuf[lt, pl.ds(t, n_seq, stride=t_len), :] = y[:, lt * lanes:(lt + 1) * lanes]
    y_ref[...] = jnp.concatenate([ybuf[lt] for lt in range(n_lt)], axis=1)
    for r in range(hist):
        nb_ref[r] = slab(r + t_len)


def _conv_step(u2, state_t, t_len, w_dw, b_dw, ln_g, ln_b):
    kw, cc = w_dw.shape
    n_seq = state_t.shape[2]
    bs = _tile(n_seq, 16)
    tb = bs * t_len
    row = lambda v: v.reshape(1, cc)
    vmem = 4 * tb * cc * 4 + 2 * tb * cc * 4 + 4 * (kw - 1) * bs * cc * 4 + tb * cc * 4 + 4 * tb * cc * 4
    hist_spec = pl.BlockSpec((None, kw - 1, bs, cc), lambda i: (0, 0, i, 0))
    return pl.pallas_call(
        functools.partial(_conv_step_kernel, kw=kw, t_len=t_len),
        grid=(n_seq // bs,),
        in_specs=[pl.BlockSpec((tb, cc), lambda i: (i, 0)),
                  pl.BlockSpec((tb, cc), lambda i: (i, 1)),
                  hist_spec,
                  pl.BlockSpec((kw, cc), lambda i: (0, 0)),
                  pl.BlockSpec((1, cc), lambda i: (0, 0)),
                  pl.BlockSpec((1, cc), lambda i: (0, 0)),
                  pl.BlockSpec((1, cc), lambda i: (0, 0))],
        out_specs=[pl.BlockSpec((tb, cc), lambda i: (i, 0)), hist_spec],
        out_shape=[jax.ShapeDtypeStruct((n_seq * t_len, cc), F32),
                   jax.ShapeDtypeStruct(state_t.shape, F32)],
        scratch_shapes=[pltpu.VMEM((cc // V7X_LANES, tb, V7X_LANES), F32),
                        pltpu.VMEM((cc // V7X_LANES, tb, V7X_LANES), F32)],
        compiler_params=_params(("arbitrary",), vmem),
        name="conv_step",
    )(u2, u2, state_t, w_dw, row(b_dw), row(ln_g), row(ln_b))


def _gla_consts(c):
    n_lev = int(np.log2(c))
    i = np.arange(c)
    mall = [(i[None, :] <= i[:, None])]
    masks = []
    for lev in range(n_lev):
        s = 1 << lev
        m = (i // (2 * s)) * (2 * s) + s - 1
        mall.append(i[None, :] <= m[:, None])
        same = (i[:, None] // (2 * s)) == (i[None, :] // (2 * s))
        upper = ((i[:, None] // s) % 2) == 1
        lower = ((i[None, :] // s) % 2) == 0
        masks.append(same & upper & lower)
    mall = jnp.asarray(np.concatenate(mall, axis=0).astype(np.float32), dtype=BF16)
    masks = jnp.asarray(np.stack(masks).astype(np.float32))
    return mall, masks, n_lev


def _gla_intra(units, mask_ref, c, n_lev):
    nt_dims = (((1,), (1,)), ((), ()))
    tn_dims = (((0,), (0,)), ((), ()))
    staged = []
    for q, k, v, ball in units:
        bc = ball[0:c]
        b_last = bc[c - 1:c, :]
        levels = []
        for lev in range(n_lev):
            bm = ball[(1 + lev) * c:(2 + lev) * c]
            levels.append(((q * jnp.exp(jnp.minimum(bc - bm, 0.0))).astype(BF16),
                           (k * jnp.exp(jnp.minimum(bm - bc, 0.0))).astype(BF16)))
        staged.append(dict(qe=(q * jnp.exp(bc)).astype(BF16), kd=(k * jnp.exp(b_last - bc)).astype(BF16),
                           vb=v.astype(BF16), diag=jnp.sum(q * k, axis=-1, keepdims=True) * v,
                           b_last=b_last, levels=levels))
    atts = []
    for u in staged:
        att = None
        for lev, (ql, kl) in enumerate(u["levels"]):
            a = lax.dot_general(ql, kl, nt_dims, preferred_element_type=F32)
            a = jnp.where(mask_ref[lev] > 0.5, a, 0.0)
            att = a if att is None else att + a
        atts.append(att)
    out = []
    for u, att in zip(staged, atts):
        dk = u["kd"].shape[1]
        dv = u["vb"].shape[1]
        o_intra = _dot(att.astype(BF16), u["vb"]) + u["diag"]
        upd = lax.dot_general(u["kd"], u["vb"], tn_dims, preferred_element_type=F32)
        dec = jnp.exp(jnp.transpose(jnp.broadcast_to(u["b_last"], (dk, dk))))
        out.append((u["qe"], o_intra, upd, jnp.tile(dec, (1, dv // dk))))
    return out


def _gla_finish(o, g, gn):
    return _rms(o, gn) * (g * _sigmoid(g))


GLA_SEQ_CHUNKS = 2
GLA_STEP_SEQS = 2


def _gla_seq_kernel(q_ref, k_ref, v_ref, g_ref, la_ref, gn_ref, mall_ref, mask_ref, y_ref, sf_ref, s_scr,
                    *, c, n_lev, heads):
    t = pl.program_id(1)
    dk = q_ref.shape[1] // heads
    dv = v_ref.shape[1] // heads
    n_ch = q_ref.shape[0] // c

    @pl.when(t == 0)
    def _():
        s_scr[...] = jnp.zeros(s_scr.shape, F32)

    mall = mall_ref[...]
    units = []
    for ci in range(n_ch):
        rows = slice(ci * c, (ci + 1) * c)
        ball = _dot_sel(mall, la_ref[rows, :])
        for h in range(heads):
            ks = slice(h * dk, (h + 1) * dk)
            units.append((q_ref[rows, ks] * (dk ** -0.5), k_ref[rows, ks], v_ref[rows, h * dv:(h + 1) * dv],
                          ball[:, ks]))
    parts = _gla_intra(units, mask_ref, c, n_lev)
    state = [s_scr[h] for h in range(heads)]
    for ci in range(n_ch):
        rows = slice(ci * c, (ci + 1) * c)
        for h in range(heads):
            vs = slice(h * dv, (h + 1) * dv)
            qe, o_intra, upd, dec = parts[ci * heads + h]
            o = o_intra + _dot(qe, state[h].astype(BF16))
            state[h] = state[h] * dec + upd
            y_ref[rows, vs] = _gla_finish(o, g_ref[rows, vs], gn_ref[...]).astype(y_ref.dtype)
    for h in range(heads):
        s_scr[h] = state[h]

    @pl.when(t == pl.num_programs(1) - 1)
    def _():
        sf_ref[...] = s_scr[...]


def _gla_seq(u2, la, n_seq, seq_len, gn_g, cc, qk):
    heads = GLA_HEADS
    dk = qk // heads
    dv = cc // heads
    c = int(np.gcd(seq_len, GLA_CHUNK))
    n_ch = GLA_SEQ_CHUNKS if (seq_len // c) % GLA_SEQ_CHUNKS == 0 else 1
    tb = n_ch * c
    nt = seq_len // tb
    mall, masks, n_lev = _gla_consts(c)
    qb = 2 * cc // qk
    vb = (2 * cc + 2 * qk) // cc
    vmem = 2 * tb * (3 * qk + 3 * cc) * 4 + 3 * heads * dk * dv * 4 + 96 * n_ch * heads * c * max(dv, 128) * 4
    return pl.pallas_call(
        functools.partial(_gla_seq_kernel, c=c, n_lev=n_lev, heads=heads),
        grid=(n_seq, nt),
        in_specs=[pl.BlockSpec((tb, qk), lambda b, t: (b * nt + t, qb)),
                  pl.BlockSpec((tb, qk), lambda b, t: (b * nt + t, qb + 1)),
                  pl.BlockSpec((tb, cc), lambda b, t: (b * nt + t, vb)),
                  pl.BlockSpec((tb, cc), lambda b, t: (b * nt + t, vb + 1)),
                  pl.BlockSpec((tb, qk), lambda b, t: (b * nt + t, 0)),
                  pl.BlockSpec((1, dv), lambda b, t: (0, 0)),
                  pl.BlockSpec(mall.shape, lambda b, t: (0, 0)),
                  pl.BlockSpec(masks.shape, lambda b, t: (0, 0, 0))],
        out_specs=[pl.BlockSpec((tb, cc), lambda b, t: (b * nt + t, 0)),
                   pl.BlockSpec((None, None, heads, dk, dv), lambda b, t: (0, b, 0, 0, 0))],
        out_shape=[jax.ShapeDtypeStruct((n_seq * seq_len, cc), BF16),
                   jax.ShapeDtypeStruct((1, n_seq, heads, dk, dv), F32)],
        scratch_shapes=[pltpu.VMEM((heads, dk, dv), F32)],
        compiler_params=_params(("arbitrary", "arbitrary"), vmem),
        name="gla_seq",
    )(u2, u2, u2, u2, la, gn_g.reshape(1, dv), mall, masks)


def _gla_step_kernel(q_ref, k_ref, v_ref, g_ref, la_ref, st_ref, gn_ref, mall_ref, mask_ref, y_ref, sn_ref,
                     *, c, n_lev, heads):
    dk = q_ref.shape[2] // heads
    dv = v_ref.shape[2] // heads
    mall = mall_ref[...]

    n_un = GLA_STEP_SEQS if q_ref.shape[0] % GLA_STEP_SEQS == 0 else 1

    def body(t, carry):
        units = []
        for ss in range(n_un):
            s = t * n_un + ss
            ball = _dot_sel(mall, la_ref[s])
            for h in range(heads):
                ks = slice(h * dk, (h + 1) * dk)
                units.append((q_ref[s, :, ks] * (dk ** -0.5), k_ref[s, :, ks], v_ref[s, :, h * dv:(h + 1) * dv],
                              ball[:, ks]))
        parts = _gla_intra(units, mask_ref, c, n_lev)
        for ss in range(n_un):
            s = t * n_un + ss
            for h in range(heads):
                vs = slice(h * dv, (h + 1) * dv)
                qe, o_intra, upd, dec = parts[ss * heads + h]
                s0 = st_ref[s, h]
                o = o_intra + _dot(qe, s0.astype(BF16))
                sn_ref[s, h] = s0 * dec + upd
                y_ref[s, :, vs] = _gla_finish(o, g_ref[s, :, vs], gn_ref[...])
        return carry

    lax.fori_loop(0, q_ref.shape[0] // n_un, body, 0)


def _gla_step(u3, la3, state, gn_g, cc, qk):
    n_seq, t_len, _ = u3.shape
    heads = GLA_HEADS
    dk = qk // heads
    dv = cc // heads
    assert t_len <= GLA_CHUNK and GLA_CHUNK % t_len == 0, "single-chunk path"
    mall, masks, n_lev = _gla_consts(t_len)
    bs = min(8, n_seq)
    qb = 2 * cc // qk
    vb = (2 * cc + 2 * qk) // cc
    vmem = 2 * bs * t_len * (3 * qk + 3 * cc) * 4 + 4 * bs * heads * dk * dv * 4 + 64 * 8 * max(dv, 128) * 4
    return pl.pallas_call(
        functools.partial(_gla_step_kernel, c=t_len, n_lev=n_lev, heads=heads),
        grid=(n_seq // bs,),
        in_specs=[pl.BlockSpec((bs, t_len, qk), lambda i: (i, 0, qb)),
                  pl.BlockSpec((bs, t_len, qk), lambda i: (i, 0, qb + 1)),
                  pl.BlockSpec((bs, t_len, cc), lambda i: (i, 0, vb)),
                  pl.BlockSpec((bs, t_len, cc), lambda i: (i, 0, vb + 1)),
                  pl.BlockSpec((bs, t_len, qk), lambda i: (i, 0, 0)),
                  pl.BlockSpec((None, bs, heads, dk, dv), lambda i: (0, i, 0, 0, 0)),
                  pl.BlockSpec((1, dv), lambda i: (0, 0)),
                  pl.BlockSpec(mall.shape, lambda i: (0, 0)),
                  pl.BlockSpec(masks.shape, lambda i: (0, 0, 0))],
        out_specs=[pl.BlockSpec((bs, t_len, cc), lambda i: (i, 0, 0)),
                   pl.BlockSpec((None, bs, heads, dk, dv), lambda i: (0, i, 0, 0, 0))],
        out_shape=[jax.ShapeDtypeStruct((n_seq, t_len, cc), F32),
                   jax.ShapeDtypeStruct((1, n_seq, heads, dk, dv), F32)],
        compiler_params=_params(("arbitrary",), vmem),
        name="gla_step",
    )(u3, u3, u3, u3, la3, state, gn_g.reshape(1, dv), mall, masks)


def _out_kernel(x_ref, yc_ref, yg_ref, wo_ref, g1_ref, sh_ref, sc_ref, n2_ref, wr_ref, br_ref,
                x1_ref, h2_ref, ti_ref, tw_ref, *, tm, seq_len):
    i = pl.program_id(0)
    cc = yc_ref.shape[1]
    n_exp = wr_ref.shape[1]
    mix = _dot(yc_ref[...].astype(BF16), wo_ref[0:cc, :]) + _dot(yg_ref[...].astype(BF16), wo_ref[cc:, :])
    x1 = x_ref[...] + _expand_rows(g1_ref, i, tm, seq_len) * mix
    x1_ref[...] = x1
    h2 = _rms(x1, n2_ref[...]) * (1.0 + _expand_rows(sc_ref, i, tm, seq_len)) + _expand_rows(sh_ref, i, tm, seq_len)
    h2_ref[...] = h2

    logits = _dot(h2.astype(BF16), wr_ref[...].astype(BF16)) + br_ref[...]
    lane = lax.broadcasted_iota(I32, logits.shape, 1)
    vals = []
    for r in range(TOP_K):
        m = jnp.max(logits, axis=-1, keepdims=True)
        idx = jnp.min(jnp.where(logits == m, lane, n_exp), axis=-1, keepdims=True)
        vals.append(m)
        ti_ref[:, r:r + 1] = idx
        logits = jnp.where(lane == idx, -jnp.inf, logits)
    ex = [jnp.exp(v - vals[0]) for v in vals]
    den = ex[0] + ex[1] + ex[2] + ex[3]
    for r in range(TOP_K):
        tw_ref[:, r:r + 1] = ex[r] / den


def _out_proj(x2, yc, yg, wo_bf, g1, sh2, sc2, n2, w_router, b_router, seq_len):
    n, d = x2.shape
    cc = yc.shape[1]
    n_seq = g1.shape[0]
    n_exp = w_router.shape[1]
    tm = _tile(n, 256)
    vmem = (6 * tm * d * 4 + 4 * tm * cc * 4 + 2 * 2 * cc * d * 2 + 8 * tm * d * 4)
    tok = lambda i: (i, 0)
    fix = lambda i: (0, 0)
    return pl.pallas_call(
        functools.partial(_out_kernel, tm=tm, seq_len=seq_len),
        grid=(n // tm,),
        in_specs=[pl.BlockSpec((tm, d), tok),
                  pl.BlockSpec((tm, cc), tok),
                  pl.BlockSpec((tm, cc), tok),
                  pl.BlockSpec((2 * cc, d), fix),
                  _table_spec(n_seq, d, tm, seq_len),
                  _table_spec(n_seq, d, tm, seq_len),
                  _table_spec(n_seq, d, tm, seq_len),
                  pl.BlockSpec((1, d), fix),
                  pl.BlockSpec((d, n_exp), fix),
                  pl.BlockSpec((1, n_exp), fix)],
        out_specs=[pl.BlockSpec((tm, d), tok),
                   pl.BlockSpec((tm, d), tok),
                   pl.BlockSpec((tm, TOP_K), tok),
                   pl.BlockSpec((tm, TOP_K), tok)],
        out_shape=[jax.ShapeDtypeStruct((n, d), F32),
                   jax.ShapeDtypeStruct((n, d), F32),
                   jax.ShapeDtypeStruct((n, TOP_K), I32),
                   jax.ShapeDtypeStruct((n, TOP_K), F32)],
        compiler_params=_params(("arbitrary",), vmem),
        name="out_proj",
    )(x2, yc, yg, wo_bf, g1, sh2, sc2, n2.reshape(1, d), w_router, b_router.reshape(1, n_exp))


def _rank_kernel(ti_ref, slot_ref, cnt_ref, off_ref, carry, offs, *, tr, n_exp, group):
    p = pl.program_id(0)
    i = pl.program_id(1)
    lane = lax.broadcasted_iota(I32, (tr, n_exp), 1)
    ti = ti_ref[...]
    hot = jnp.zeros((tr, n_exp), F32)
    for k in range(TOP_K):
        hot = hot + (lane == ti[:, k:k + 1]).astype(F32)

    @pl.when((p == 0) & (i == 0))
    def _():
        carry[...] = jnp.zeros(carry.shape, F32)

    @pl.when((p == 1) & (i == 0))
    def _():
        cnt = carry[0:1, :]
        padded = jnp.ceil(cnt * (1.0 / group)) * group
        r = lax.broadcasted_iota(I32, (n_exp, n_exp), 0)
        c = lax.broadcasted_iota(I32, (n_exp, n_exp), 1)
        before = (r < c).astype(BF16)
        hi, mid, lo = _split3(jnp.broadcast_to(padded, (8, n_exp)))
        off = _dot(hi, before) + _dot(mid, before) + _dot(lo, before)
        offs[...] = off
        cnt_ref[...] = cnt.astype(I32)
        off_ref[...] = off[0:1, :].astype(I32)
        carry[...] = jnp.zeros(carry.shape, F32)

    @pl.when(p == 1)
    def _():
        r = lax.broadcasted_iota(I32, (tr, tr), 0)
        c = lax.broadcasted_iota(I32, (tr, tr), 1)
        earlier = (c < r).astype(BF16)
        tot = _dot(earlier, hot.astype(BF16)) + carry[0:1, :] + offs[0:1, :]
        for k in range(TOP_K):
            sel = jnp.where(lane == ti[:, k:k + 1], tot, 0.0)
            slot_ref[:, k:k + 1] = jnp.sum(sel, axis=-1, keepdims=True).astype(I32)

    carry[...] = carry[...] + jnp.sum(hot, axis=0, keepdims=True)


def _rank(top_i, n_exp, group):
    n = top_i.shape[0]
    tr = _tile(n, 256)
    return pl.pallas_call(
        functools.partial(_rank_kernel, tr=tr, n_exp=n_exp, group=group),
        grid=(2, n // tr),
        in_specs=[pl.BlockSpec((tr, TOP_K), lambda p, i: (i, 0))],
        out_specs=[pl.BlockSpec((tr, TOP_K), lambda p, i: (i * p, 0)),
                   pl.BlockSpec((1, n_exp), lambda p, i: (0, 0)),
                   pl.BlockSpec((1, n_exp), lambda p, i: (0, 0))],
        out_shape=[jax.ShapeDtypeStruct((n, TOP_K), I32),
                   jax.ShapeDtypeStruct((1, n_exp), I32),
                   jax.ShapeDtypeStruct((1, n_exp), I32)],
        scratch_shapes=[pltpu.VMEM((8, n_exp), F32), pltpu.VMEM((8, n_exp), F32)],
        compiler_params=_params(("arbitrary", "arbitrary"), 8 * tr * tr * 4),
        name="rank",
    )(top_i)


def _scatter_rows(slot_ref, h_ref, xs_ref, sem, td):
    def issue(r, carry):
        for k in range(TOP_K):
            s = slot_ref[0, r * TOP_K + k]
            pltpu.make_async_copy(h_ref.at[pl.ds(r, 1), :], xs_ref.at[pl.ds(s, 1), :], sem).start()
        return carry

    lax.fori_loop(0, td, issue, 0)
    for k in range(TOP_K):
        pltpu.make_async_copy(h_ref, xs_ref.at[pl.ds(0, td), :], sem).wait()


def _dispatch_kernel(cnt, off, slot_ref, ha_ref, hb_ref, xs_ref, zbuf, sem, zsem, *, td, group, n_a):
    n_exp = cnt.shape[0]
    n_rows = xs_ref.shape[0]
    i = pl.program_id(0)

    def padded(e):
        return (cnt[e] + group - 1) // group * group

    def z_row(row):
        return pltpu.make_async_copy(zbuf.at[pl.ds(0, 1), :], xs_ref.at[pl.ds(row, 1), :], zsem)

    def z_block(row):
        return pltpu.make_async_copy(zbuf, xs_ref.at[pl.ds(pl.multiple_of(row, group), group), :], zsem)

    def for_pad_rows(fn):
        def expert(e, carry):
            lax.fori_loop(off[e] + cnt[e], off[e] + padded(e), lambda r, c: (fn(r), c)[1], 0)
            return carry
        lax.fori_loop(0, n_exp, expert, 0)

    def for_tail_blocks(fn):
        first = (off[n_exp - 1] + padded(n_exp - 1)) // group
        lax.fori_loop(first, n_rows // group, lambda b, c: (fn(b * group), c)[1], 0)

    @pl.when(i == 0)
    def _():
        zbuf[...] = jnp.zeros(zbuf.shape, zbuf.dtype)
        for_pad_rows(lambda r: z_row(r).start())
        for_tail_blocks(lambda r: z_block(r).start())

    @pl.when(i < n_a)
    def _():
        _scatter_rows(slot_ref, ha_ref, xs_ref, sem, td)

    @pl.when(i >= n_a)
    def _():
        _scatter_rows(slot_ref, hb_ref, xs_ref, sem, td)

    @pl.when(i == pl.num_programs(0) - 1)
    def _():
        for_pad_rows(lambda r: z_row(0).wait())
        for_tail_blocks(lambda r: z_block(0).wait())


def _dispatch(slots, h_a, h_b, cnt, off, n_rows, group):
    d = h_a.shape[1]
    td = _tile(int(np.gcd(h_a.shape[0], h_b.shape[0])), 256)
    n_a, n_b = h_a.shape[0] // td, h_b.shape[0] // td
    grid_spec = pltpu.PrefetchScalarGridSpec(
        num_scalar_prefetch=2,
        grid=(n_a + n_b,),
        in_specs=[pl.BlockSpec((None, 1, td * TOP_K), lambda i, *_: (i, 0, 0), memory_space=pltpu.SMEM),
                  pl.BlockSpec((td, d), lambda i, *_: (jnp.minimum(i, n_a - 1), 0)),
                  pl.BlockSpec((td, d), lambda i, *_: (jnp.maximum(i - n_a, 0), 0))],
        out_specs=pl.BlockSpec(memory_space=pl.ANY),
        scratch_shapes=[pltpu.VMEM((group, d), h_a.dtype), pltpu.SemaphoreType.DMA, pltpu.SemaphoreType.DMA])
    return pl.pallas_call(
        functools.partial(_dispatch_kernel, td=td, group=group, n_a=n_a),
        grid_spec=grid_spec,
        out_shape=jax.ShapeDtypeStruct((n_rows, d), h_a.dtype),
        compiler_params=_params(("arbitrary",), 4 * td * d * 4 + group * d * 4),
        name="dispatch",
    )(cnt, off, slots.reshape(n_a + n_b, 1, td * TOP_K), h_a, h_b)


MOE_CHUNK = 2048
MOE_SUB = 256
MOE_TF = 256
MOE_TN = 256


def _for_sub_blocks(n, body):
    def trips(width, first, count):
        def trip(t, carry):
            for u in range(width):
                body(first + t * width + u)
            return carry
        lax.fori_loop(0, count, trip, 0)

    trips(4, 0, n // 4)

    @pl.when((n // 2) % 2 == 1)
    def _():
        body((n // 4) * 4)
        body((n // 4) * 4 + 1)

    @pl.when(n % 2 == 1)
    def _():
        body(n - 1)


def _moe_kernel(che, chrow, chn, tail, xs_ref, wg_ref, wu_ref, bg_ref, bu_ref, wd_ref, bd_ref, yb_ref,
                xbuf, act, wgb, wub, wdb, obuf, semx, semo, *, n_f, n_n, n_chunks):
    c = pl.program_id(0)
    j = pl.program_id(1)
    nsb = chn[c]
    sub = MOE_SUB

    def x_copy(cc, i):
        row = pl.multiple_of(chrow[cc] + i * sub, sub)
        return pltpu.make_async_copy(xs_ref.at[pl.ds(row, sub), :],
                                     xbuf.at[pl.ds(pl.multiple_of(i * sub, sub), sub), :], semx)

    def x_fetch(cc):
        lax.fori_loop(0, chn[cc], lambda i, carry: (x_copy(cc, i).start(), carry)[1], 0)

    def o_copy(slot, i, jj):
        row = pl.multiple_of(chrow[c] + i * sub, sub)
        col = pl.multiple_of(jj * MOE_TN, MOE_TN)
        return pltpu.make_async_copy(obuf.at[slot, pl.ds(pl.multiple_of(i * sub, sub), sub), :],
                                     yb_ref.at[pl.ds(row, sub), pl.ds(col, MOE_TN)], semo.at[slot])

    @pl.when((c == 0) & (j == 0))
    def _():
        x_fetch(0)

    @pl.when(j == 0)
    def _():
        lax.fori_loop(0, nsb, lambda i, carry: (x_copy(c, 0).wait(), carry)[1], 0)

    @pl.when((j == n_f) & (c + 1 < n_chunks))
    def _():
        x_fetch(jnp.minimum(c + 1, n_chunks - 1))

    @pl.when((j < n_f) & (nsb > 0))
    def _():
        wgb[...] = wg_ref[...].astype(BF16)
        wub[...] = wu_ref[...].astype(BF16)

        def sub_block(i):
            rows = pl.ds(pl.multiple_of(i * sub, sub), sub)
            x = xbuf[rows, :].astype(BF16)
            gate = jnp.minimum(_dot(x, wgb[...]) + bg_ref[...], SWIGLU_LIMIT)
            up = jnp.clip(_dot(x, wub[...]) + bu_ref[...], -SWIGLU_LIMIT, SWIGLU_LIMIT)
            glu = gate * _sigmoid(SWIGLU_ALPHA * gate)
            act[j, rows, :] = ((up + 1.0) * glu).astype(BF16)

        _for_sub_blocks(nsb, sub_block)

    @pl.when((j >= n_f) & (nsb > 0))
    def _():
        jj = j - n_f
        slot = jj % 2
        wdb[...] = wd_ref[...].astype(BF16)

        def retire(s):
            def one(i, carry):
                o_copy(s, 0, 0).wait()
                return carry
            lax.fori_loop(0, nsb, one, 0)

        @pl.when(jj >= 2)
        def _():
            retire(slot)

        def sub_block(i):
            rows = pl.ds(pl.multiple_of(i * sub, sub), sub)
            a = jnp.concatenate([act[f, rows, :] for f in range(n_f)], axis=1)
            obuf[slot, rows, :] = _dot(a, wdb[...]) + bd_ref[...]
            o_copy(slot, i, jj).start()

        _for_sub_blocks(nsb, sub_block)

        @pl.when(jj == n_n - 1)
        def _():
            retire(slot)
            if n_n >= 2:
                retire(1 - slot)

    @pl.when((c == n_chunks - 1) & (j == n_f + n_n - 1))
    def _():
        n_tail = (yb_ref.shape[0] - tail[0]) // sub
        obuf[0, 0:sub, :] = jnp.zeros((sub, MOE_TN), F32)

        def z_copy(i, js):
            row = pl.multiple_of(tail[0] + i * sub, sub)
            return pltpu.make_async_copy(obuf.at[0, pl.ds(0, sub), :],
                                         yb_ref.at[pl.ds(row, sub), pl.ds(js * MOE_TN, MOE_TN)], semo.at[0])

        def fill(i, carry):
            for js in range(n_n):
                z_copy(i, js).start()
            return carry

        def drain(i, carry):
            for js in range(n_n):
                z_copy(0, js).wait()
            return carry

        lax.fori_loop(0, n_tail, fill, 0)
        lax.fori_loop(0, n_tail, drain, 0)


def _moe(xs, yb_rows, ch_e, ch_row, ch_n, tail, w_gate, b_gate, w_up, b_up, w_down, b_down):
    n_exp, d, f = w_gate.shape
    n_f = f // MOE_TF
    n_n = d // MOE_TN
    n_chunks = ch_e.shape[0]

    def w1(c, j, che, chrow, chn, tail):
        return (che[c], 0, jnp.where(chn[c] > 0, jnp.minimum(j, n_f - 1), n_f - 1))

    def w2(c, j, che, chrow, chn, tail):
        return (che[c], 0, jnp.where(chn[c] > 0, jnp.maximum(j - n_f, 0), n_n - 1))

    vmem = (MOE_CHUNK * d * 4 + MOE_CHUNK * f * 2
            + 2 * 2 * d * MOE_TF * 4 + 2 * f * MOE_TN * 4 + 2 * d * MOE_TF * 2 + f * MOE_TN * 2
            + 2 * MOE_CHUNK * MOE_TN * 4 + 8 * MOE_SUB * max(d, f) * 4)
    grid_spec = pltpu.PrefetchScalarGridSpec(
        num_scalar_prefetch=4,
        grid=(n_chunks, n_f + n_n),
        in_specs=[pl.BlockSpec(memory_space=pl.ANY),
                  pl.BlockSpec((None, d, MOE_TF), w1),
                  pl.BlockSpec((None, d, MOE_TF), w1),
                  pl.BlockSpec((None, 1, MOE_TF), w1),
                  pl.BlockSpec((None, 1, MOE_TF), w1),
                  pl.BlockSpec((None, f, MOE_TN), w2),
                  pl.BlockSpec((None, 1, MOE_TN), w2)],
        out_specs=pl.BlockSpec(memory_space=pl.ANY),
        scratch_shapes=[pltpu.VMEM((MOE_CHUNK, d), F32),
                        pltpu.VMEM((n_f, MOE_CHUNK, MOE_TF), BF16),
                        pltpu.VMEM((d, MOE_TF), BF16),
                        pltpu.VMEM((d, MOE_TF), BF16),
                        pltpu.VMEM((f, MOE_TN), BF16),
                        pltpu.VMEM((2, MOE_CHUNK, MOE_TN), F32),
                        pltpu.SemaphoreType.DMA,
                        pltpu.SemaphoreType.DMA((2,))])
    return pl.pallas_call(
        functools.partial(_moe_kernel, n_f=n_f, n_n=n_n, n_chunks=n_chunks),
        grid_spec=grid_spec,
        out_shape=jax.ShapeDtypeStruct((yb_rows, d), F32),
        compiler_params=_params(("arbitrary", "arbitrary"), vmem),
        name="moe",
    )(ch_e, ch_row, ch_n, tail, xs, w_gate, w_up, b_gate.reshape(n_exp, 1, f), b_up.reshape(n_exp, 1, f),
      w_down, b_down.reshape(n_exp, 1, d))


def _chunk_table(cnt, off, n_chunks):
    n_exp = cnt.shape[0]
    per = MOE_CHUNK // MOE_SUB
    nsb = (cnt + MOE_SUB - 1) // MOE_SUB
    nch = (nsb + per - 1) // per
    end = jnp.cumsum(nch)
    start = end - nch
    cidx = jnp.arange(n_chunks, dtype=I32)
    owner = lambda ci: jnp.minimum(jnp.sum((end[None, :] <= ci[:, None]).astype(I32), axis=1), n_exp - 1)
    e = owner(cidx)
    used = cidx < end[-1]
    last_e = owner(end[-1:] - 1)[0]
    local = cidx - start[e]
    ch_e = jnp.where(used, e, last_e)
    ch_row = jnp.where(used, off[e] + local * MOE_CHUNK, 0).astype(I32)
    ch_n = jnp.where(used, jnp.minimum(per, nsb[e] - local * per), 0).astype(I32)
    tail = (off[-1] + nsb[-1] * MOE_SUB).astype(I32).reshape(1)
    return ch_e, ch_row, ch_n, tail


def _combine_kernel(slot_ref, next_slot_ref, tw_ref, x1_ref, g2_ref, gf_ref, yb_ref, y_ref, gbuf, sem,
                    *, tc, seq_len):
    i = pl.program_id(0)
    cur = i % 2

    def gather(s_ref, half):
        def rows(r, carry):
            for k in range(TOP_K):
                s = s_ref[0, r * TOP_K + k]
                pltpu.make_async_copy(yb_ref.at[pl.ds(s, 1), :], gbuf.at[half, k, pl.ds(r, 1), :],
                                      sem.at[half]).start()
            return carry
        lax.fori_loop(0, tc, rows, 0)

    @pl.when(i == 0)
    def _():
        gather(slot_ref, 0)

    @pl.when(i + 1 < pl.num_programs(0))
    def _():
        gather(next_slot_ref, 1 - cur)

    for k in range(TOP_K):
        pltpu.make_async_copy(yb_ref.at[pl.ds(0, tc), :], gbuf.at[cur, k], sem.at[cur]).wait()
    tw = tw_ref[...]
    y = tw[:, 0:1] * gbuf[cur, 0]
    for k in range(1, TOP_K):
        y = y + tw[:, k:k + 1] * gbuf[cur, k]
    x2 = x1_ref[...] + _expand_rows(g2_ref, i, tc, seq_len) * y
    y_ref[...] = _rms(x2, gf_ref[...])


def _combine(slots, top_w, x1, g2, gf, yb, seq_len):
    n, d = x1.shape
    n_seq = g2.shape[0]
    tc = _tile(n, 128)
    nt = n // tc
    vmem = 2 * TOP_K * tc * d * 4 + 4 * tc * d * 4 + 6 * tc * d * 4
    slots3 = slots.reshape(nt, 1, tc * TOP_K)
    return pl.pallas_call(
        functools.partial(_combine_kernel, tc=tc, seq_len=seq_len),
        grid=(nt,),
        in_specs=[pl.BlockSpec((None, 1, tc * TOP_K), lambda i: (i, 0, 0), memory_space=pltpu.SMEM),
                  pl.BlockSpec((None, 1, tc * TOP_K), lambda i: (jnp.minimum(i + 1, nt - 1), 0, 0),
                               memory_space=pltpu.SMEM),
                  pl.BlockSpec((tc, TOP_K), lambda i: (i, 0)),
                  pl.BlockSpec((tc, d), lambda i: (i, 0)),
                  _table_spec(n_seq, d, tc, seq_len),
                  pl.BlockSpec((1, d), lambda i: (0, 0)),
                  pl.BlockSpec(memory_space=pl.ANY)],
        out_specs=pl.BlockSpec((tc, d), lambda i: (i, 0)),
        out_shape=jax.ShapeDtypeStruct((n, d), F32),
        scratch_shapes=[pltpu.VMEM((2, TOP_K, tc, d), F32), pltpu.SemaphoreType.DMA((2,))],
        compiler_params=_params(("arbitrary",), vmem),
        name="combine",
    )(slots3, slots3, top_w, x1, g2, gf.reshape(1, d), yb)


def _layer(xp, xs, cp, cs, conv_state, gla_state, w):
    (w_ada, b_ada, norm1_g, w_in, w_dw, b_dw, conv_ln_g, conv_ln_b, w_alpha, b_alpha, gla_norm_g, w_out,
     norm2_g, w_router, b_router, w_gate, b_gate, w_up, b_up, w_down, b_down, final_g) = w
    bp, tp, d = xp.shape
    bs, ts, _ = xs.shape
    cc = w_dw.shape[1]
    qk = w_alpha.shape[1]
    n_exp = w_router.shape[1]
    n_main = 2 * cc + 2 * qk + 2 * cc
    np_, ns_ = bp * tp, bs * ts

    mod = _ada(jnp.concatenate([cp, cs], axis=0), w_ada, b_ada)
    mods_p = [mod[:bp, i * d:(i + 1) * d] for i in range(6)]
    mods_s = [mod[bp:, i * d:(i + 1) * d] for i in range(6)]
    w_in_t = w_in.T
    w_low = w_in_t[n_main:]
    w_main_bf = w_in_t[:n_main].astype(BF16)
    wo_bf = w_out.astype(BF16)

    xp2 = xp.reshape(np_, d)
    xs2 = xs.reshape(ns_, d)

    u_p, la_p = _in_proj(xp2, mods_p[0], mods_p[1], norm1_g, w_main_bf, w_low, w_alpha, b_alpha, tp)
    yc_p, conv_p = _conv_seq(u_p, bp, tp, w_dw, b_dw, conv_ln_g, conv_ln_b)
    yg_p, gla_p = _gla_seq(u_p, la_p, bp, tp, gla_norm_g, cc, qk)
    x1_p, h2_p, ti_p, tw_p = _out_proj(xp2, yc_p, yg_p, wo_bf, mods_p[2], mods_p[3], mods_p[4], norm2_g,
                                       w_router, b_router, tp)

    u_s, la_s = _in_proj(xs2, mods_s[0], mods_s[1], norm1_g, w_main_bf, w_low, w_alpha, b_alpha, ts)
    yc_s, conv_s = _conv_step(u_s, jnp.transpose(conv_state, (0, 2, 1, 3)), ts, w_dw, b_dw, conv_ln_g, conv_ln_b)
    conv_s = jnp.transpose(conv_s, (0, 2, 1, 3))
    yg_s, gla_s = _gla_step(u_s.reshape(bs, ts, n_main), la_s.reshape(bs, ts, qk), gla_state, gla_norm_g, cc, qk)
    x1_s, h2_s, ti_s, tw_s = _out_proj(xs2, yc_s, yg_s.reshape(ns_, cc), wo_bf, mods_s[2],
                                       mods_s[3], mods_s[4], norm2_g, w_router, b_router, ts)

    n_asg = (np_ + ns_) * TOP_K
    slots, cnt, off = _rank(jnp.concatenate([ti_p, ti_s], axis=0), n_exp, MOE_SUB)
    rows = -(-(n_asg + n_exp * MOE_SUB) // MOE_SUB) * MOE_SUB
    xsort = _dispatch(slots, h2_p, h2_s, cnt[0], off[0], rows, MOE_SUB)
    n_chunks = (n_asg + n_exp * MOE_SUB) // MOE_CHUNK + n_exp
    ch_e, ch_row, ch_n, tail = _chunk_table(cnt[0], off[0], n_chunks)
    yb = _moe(xsort, rows, ch_e, ch_row, ch_n, tail, w_gate, b_gate, w_up, b_up, w_down, b_down)

    y_p = _combine(slots[:np_], tw_p, x1_p, mods_p[5], final_g, yb, tp)
    y_s = _combine(slots[np_:], tw_s, x1_s, mods_s[5], final_g, yb, ts)
    return y_p.reshape(bp, tp, d), y_s.reshape(bs, ts, d), conv_p, gla_p, conv_s, gla_s


def kernel(x_prompt, x_sample, c_prompt, c_sample, state_conv, state_gla, w_ada, b_ada, norm1_g, w_in, w_dw, b_dw,
           conv_ln_g, conv_ln_b, w_alpha, b_alpha, gla_norm_g, w_out, norm2_g, w_router, b_router, w_gate, b_gate,
           w_up, b_up, w_down, b_down, final_norm_g):
    assert w_ada.shape[0] == 1, "the final norm is fused into the (single) layer"
    first = lambda a: a.reshape(a.shape[1:])
    w = tuple(first(a) for a in (w_ada, b_ada, norm1_g, w_in, w_dw, b_dw, conv_ln_g, conv_ln_b, w_alpha, b_alpha,
                                 gla_norm_g, w_out, norm2_g, w_router, b_router, w_gate, b_gate, w_up, b_up,
                                 w_down, b_down)) + (final_norm_g,)
    return _layer(x_prompt, x_sample, c_prompt, c_sample, state_conv, state_gla, w)
```

```python
import functools

import numpy as np
import jax
import jax.numpy as jnp
from jax import lax
from jax.experimental import pallas as pl
from jax.experimental.pallas import tpu as pltpu

F32 = jnp.float32
BF16 = jnp.bfloat16
I32 = jnp.int32

EPS = 1e-5
GLA_HEADS = 4
GLA_GATE_NORM = 16.0
GLA_CHUNK = 64
TOP_K = 4
SWIGLU_LIMIT = 7.0
SWIGLU_ALPHA = 1.702

V7X_LANES = 128
V7X_VMEM_BYTES = 64 * 1024 * 1024
VMEM_CAP_BYTES = V7X_VMEM_BYTES - 6 * 1024 * 1024


def _params(semantics, vmem_bytes):
    limit = int(min(VMEM_CAP_BYTES, max(vmem_bytes * 5 // 4 + (4 << 20), 16 << 20)))
    return pltpu.CompilerParams(dimension_semantics=semantics, vmem_limit_bytes=limit)


def _tile(n, want):
    t = min(want, n)
    while n % t or t % 8:
        t -= 8 if t % 8 == 0 else t % 8
        assert t > 0, (n, want)
    return t


NT_DIMS = (((1,), (1,)), ((), ()))


def _dot(a, b):
    return jnp.dot(a, b, preferred_element_type=F32)


def _split3(v):
    hi = v.astype(BF16)
    r = v - hi.astype(F32)
    mid = r.astype(BF16)
    lo = (r - mid.astype(F32)).astype(BF16)
    return hi, mid, lo


def _dot_sel(m01, v):
    hi, mid, lo = _split3(v)
    return _dot(m01, hi) + _dot(m01, mid) + _dot(m01, lo)


def _sigmoid(x):
    return jax.nn.sigmoid(x)


def _rms(x, g):
    return x * lax.rsqrt(jnp.mean(x * x, axis=-1, keepdims=True) + EPS) * g


def _expand_rows(m_ref, i, tm, seq_len):
    if seq_len >= tm:
        return m_ref[pl.ds((i * tm) // seq_len, 1), :]
    nseq = tm // seq_len
    r = lax.broadcasted_iota(I32, (tm, nseq), 0)
    c = lax.broadcasted_iota(I32, (tm, nseq), 1) * seq_len
    sel = ((r >= c) & (r < c + seq_len)).astype(BF16)
    return _dot_sel(sel, m_ref[...])


def _table_spec(n_seq, d, tm, seq_len):
    if seq_len >= tm:
        return pl.BlockSpec((n_seq, d), lambda i, *_: (0, 0))
    return pl.BlockSpec((tm // seq_len, d), lambda i, *_: (i, 0))


def _ada_kernel(c_ref, w_ref, b_ref, o_ref):
    c = c_ref[...]
    s = (c * _sigmoid(c)).astype(BF16)
    o_ref[...] = _dot(s, w_ref[...].astype(BF16)) + b_ref[...]


def _ada(c_all, w_ada, b_ada):
    bc, d = c_all.shape
    n6 = w_ada.shape[1]
    tn = min(1024, n6)
    return pl.pallas_call(
        _ada_kernel,
        grid=(n6 // tn,),
        in_specs=[pl.BlockSpec((bc, d), lambda j: (0, 0)),
                  pl.BlockSpec((d, tn), lambda j: (0, j)),
                  pl.BlockSpec((1, tn), lambda j: (0, j))],
        out_specs=pl.BlockSpec((bc, tn), lambda j: (0, j)),
        out_shape=jax.ShapeDtypeStruct((bc, n6), F32),
        compiler_params=_params(("arbitrary",), 2 * d * tn * 4 + d * tn * 2 + 4 * bc * (d + tn) * 4),
        name="ada",
    )(c_all, w_ada, b_ada.reshape(1, n6))


IN_TN = 512


def _in_kernel(x_ref, sh_ref, sc_ref, g_ref, w_hbm, wlow_ref, wal_ref, bal_ref, u_ref, la_ref, w_vmem, sem,
               *, tm, seq_len):
    i = pl.program_id(0)
    load_w = pltpu.make_async_copy(w_hbm, w_vmem, sem)

    @pl.when(i == 0)
    def _():
        load_w.start()

    xn = _rms(x_ref[...], g_ref[...])
    h = xn * (1.0 + _expand_rows(sc_ref, i, tm, seq_len)) + _expand_rows(sh_ref, i, tm, seq_len)
    hb = h.astype(BF16)
    a_low = lax.dot_general(hb, wlow_ref[...].astype(BF16), NT_DIMS, preferred_element_type=F32)
    z = _dot(a_low.astype(BF16), wal_ref[...].astype(BF16)) + bal_ref[...]
    la_ref[...] = (jnp.minimum(z, 0.0) - jnp.log(1.0 + jnp.exp(-jnp.abs(z)))) * (1.0 / GLA_GATE_NORM)

    @pl.when(i == 0)
    def _():
        load_w.wait()

    for jn in range(u_ref.shape[1] // IN_TN):
        cols = slice(jn * IN_TN, (jn + 1) * IN_TN)
        u_ref[:, cols] = lax.dot_general(hb, w_vmem[cols, :], NT_DIMS, preferred_element_type=F32)


def _in_proj(x2, sh, sc, g, w_main_bf, w_low, w_alpha, b_alpha, seq_len):
    n, d = x2.shape
    n_main = w_main_bf.shape[0]
    n_seq = sh.shape[0]
    qk = w_alpha.shape[1]
    rank = w_low.shape[0]
    tm = _tile(n, 256)
    vmem = d * n_main * 2 + 2 * tm * d * 4 + 2 * tm * n_main * 4 + 2 * tm * qk * 4 + 4 * tm * d * 4
    fix = lambda i: (0, 0)
    return pl.pallas_call(
        functools.partial(_in_kernel, tm=tm, seq_len=seq_len),
        grid=(n // tm,),
        in_specs=[pl.BlockSpec((tm, d), lambda i: (i, 0)),
                  _table_spec(n_seq, d, tm, seq_len),
                  _table_spec(n_seq, d, tm, seq_len),
                  pl.BlockSpec((1, d), fix),
                  pl.BlockSpec(memory_space=pl.ANY),
                  pl.BlockSpec((rank, d), fix),
                  pl.BlockSpec((rank, qk), fix),
                  pl.BlockSpec((1, qk), fix)],
        out_specs=[pl.BlockSpec((tm, n_main), lambda i: (i, 0)),
                   pl.BlockSpec((tm, qk), lambda i: (i, 0))],
        out_shape=[jax.ShapeDtypeStruct((n, n_main), F32), jax.ShapeDtypeStruct((n, qk), F32)],
        scratch_shapes=[pltpu.VMEM((n_main, d), BF16), pltpu.SemaphoreType.DMA],
        compiler_params=_params(("arbitrary",), vmem),
        name="in_proj",
    )(x2, sh, sc, g.reshape(1, d), w_main_bf, w_low, w_alpha, b_alpha.reshape(1, qk))


CONV_HALO = 32


def _ln_silu(y, g, b):
    mu = jnp.mean(y, axis=-1, keepdims=True)
    yc = y - mu
    var = jnp.mean(yc * yc, axis=-1, keepdims=True)
    yn = yc * lax.rsqrt(var + EPS) * g + b
    return yn * _sigmoid(yn)


def _conv_seq_kernel(ua_ref, ub_ref, w_ref, b_ref, lg_ref, lb_ref, y_ref, nb_ref, buf, shifted, *, tt, kw):
    t = pl.program_id(1)
    cc = buf.shape[1]

    @pl.when(t == 0)
    def _():
        buf[0:CONV_HALO, :] = jnp.zeros((CONV_HALO, cc), F32)

    buf[CONV_HALO:CONV_HALO + tt, :] = ua_ref[...] * _sigmoid(ub_ref[...])
    span = shifted.shape[1]
    for r in range(1, 8):
        shifted[r - 1] = buf[r:r + span, :]
    first = CONV_HALO - (kw - 1)

    acc = jnp.broadcast_to(b_ref[...], (tt, cc))
    for j in range(kw):
        q, r = divmod(first + j, 8)
        win = buf[8 * q:8 * q + tt, :] if r == 0 else shifted[r - 1, 8 * q:8 * q + tt, :]
        acc = acc + w_ref[j:j + 1, :] * win
    y_ref[...] = _ln_silu(acc, lg_ref[...], lb_ref[...]).astype(y_ref.dtype)

    @pl.when(t == pl.num_programs(1) - 1)
    def _():
        nb_ref[...] = buf[CONV_HALO + tt - (kw - 1):CONV_HALO + tt, :]

    buf[0:CONV_HALO, :] = buf[tt:tt + CONV_HALO, :]


def _conv_seq(u2, n_seq, seq_len, w_dw, b_dw, ln_g, ln_b):
    kw, cc = w_dw.shape
    assert kw - 1 <= CONV_HALO
    tt = _tile(seq_len, 256)
    assert tt % 8 == 0 and tt >= CONV_HALO
    nt = seq_len // tt
    span = tt + CONV_HALO - 8
    row = lambda v: v.reshape(1, cc)
    vmem = 4 * tt * cc * 4 + 2 * tt * cc * 2 + (CONV_HALO + tt + 7 * span) * cc * 4 + 6 * tt * cc * 4
    return pl.pallas_call(
        functools.partial(_conv_seq_kernel, tt=tt, kw=kw),
        grid=(n_seq, nt),
        in_specs=[pl.BlockSpec((tt, cc), lambda b, t: (b * nt + t, 0)),
                  pl.BlockSpec((tt, cc), lambda b, t: (b * nt + t, 1)),
                  pl.BlockSpec((kw, cc), lambda b, t: (0, 0)),
                  pl.BlockSpec((1, cc), lambda b, t: (0, 0)),
                  pl.BlockSpec((1, cc), lambda b, t: (0, 0)),
                  pl.BlockSpec((1, cc), lambda b, t: (0, 0))],
        out_specs=[pl.BlockSpec((tt, cc), lambda b, t: (b * nt + t, 0)),
                   pl.BlockSpec((None, None, kw - 1, cc), lambda b, t: (0, b, 0, 0))],
        out_shape=[jax.ShapeDtypeStruct((n_seq * seq_len, cc), BF16),
                   jax.ShapeDtypeStruct((1, n_seq, kw - 1, cc), F32)],
        scratch_shapes=[pltpu.VMEM((CONV_HALO + tt, cc), F32), pltpu.VMEM((7, span, cc), F32)],
        compiler_params=_params(("arbitrary", "arbitrary"), vmem),
        name="conv_seq",
    )(u2, u2, w_dw, row(b_dw), row(ln_g), row(ln_b))


def _conv_step_kernel(ua_ref, ub_ref, st_ref, w_ref, b_ref, lg_ref, lb_ref, y_ref, nb_ref, gbuf, ybuf,
                      *, kw, t_len):
    hist = kw - 1
    n_seq, cc = st_ref.shape[1], st_ref.shape[2]
    n_lt = gbuf.shape[0]
    lanes = gbuf.shape[2]
    glu = ua_ref[...] * _sigmoid(ub_ref[...])
    for lt in range(n_lt):
        gbuf[lt] = glu[:, lt * lanes:(lt + 1) * lanes]

    def slab(r):
        if r < hist:
            return st_ref[r]
        step = pl.ds(r - hist, n_seq, stride=t_len)
        return jnp.concatenate([gbuf[lt, step, :] for lt in range(n_lt)], axis=1)

    for t in range(t_len):
        acc = jnp.broadcast_to(b_ref[...], (n_seq, cc))
        for j in range(kw):
            acc = acc + w_ref[j:j + 1, :] * slab(t + j)
        y = _ln_silu(acc, lg_ref[...], lb_ref[...])
        for lt in range(n_lt):
            ybuf[lt, pl.ds(t, n_seq, stride=t_len), :] = y[:, lt * lanes:(lt + 1) * lanes]
    y_ref[...] = jnp.concatenate([ybuf[lt] for lt in range(n_lt)], axis=1)
    for r in range(hist):
        nb_ref[r] = slab(r + t_len)


def _conv_step(u2, state_t, t_len, w_dw, b_dw, ln_g, ln_b):
    kw, cc = w_dw.shape
    n_seq = state_t.shape[2]
    bs = _tile(n_seq, 16)
    tb = bs * t_len
    row = lambda v: v.reshape(1, cc)
    vmem = 4 * tb * cc * 4 + 2 * tb * cc * 4 + 4 * (kw - 1) * bs * cc * 4 + tb * cc * 4 + 4 * tb * cc * 4
    hist_spec = pl.BlockSpec((None, kw - 1, bs, cc), lambda i: (0, 0, i, 0))
    return pl.pallas_call(
        functools.partial(_conv_step_kernel, kw=kw, t_len=t_len),
        grid=(n_seq // bs,),
        in_specs=[pl.BlockSpec((tb, cc), lambda i: (i, 0)),
                  pl.BlockSpec((tb, cc), lambda i: (i, 1)),
                  hist_spec,
                  pl.BlockSpec((kw, cc), lambda i: (0, 0)),
                  pl.BlockSpec((1, cc), lambda i: (0, 0)),
                  pl.BlockSpec((1, cc), lambda i: (0, 0)),
                  pl.BlockSpec((1, cc), lambda i: (0, 0))],
        out_specs=[pl.BlockSpec((tb, cc), lambda i: (i, 0)), hist_spec],
        out_shape=[jax.ShapeDtypeStruct((n_seq * t_len, cc), F32),
                   jax.ShapeDtypeStruct(state_t.shape, F32)],
        scratch_shapes=[pltpu.VMEM((cc // V7X_LANES, tb, V7X_LANES), F32),
                        pltpu.VMEM((cc // V7X_LANES, tb, V7X_LANES), F32)],
        compiler_params=_params(("arbitrary",), vmem),
        name="conv_step",
    )(u2, u2, state_t, w_dw, row(b_dw), row(ln_g), row(ln_b))


def _gla_consts(c):
    n_lev = int(np.log2(c))
    i = np.arange(c)
    mall = [(i[None, :] <= i[:, None])]
    masks = []
    for lev in range(n_lev):
        s = 1 << lev
        m = (i // (2 * s)) * (2 * s) + s - 1
        mall.append(i[None, :] <= m[:, None])
        same = (i[:, None] // (2 * s)) == (i[None, :] // (2 * s))
        upper = ((i[:, None] // s) % 2) == 1
        lower = ((i[None, :] // s) % 2) == 0
        masks.append(same & upper & lower)
    mall = jnp.asarray(np.concatenate(mall, axis=0).astype(np.float32), dtype=BF16)
    masks = jnp.asarray(np.stack(masks).astype(np.float32))
    return mall, masks, n_lev


def _gla_intra(units, mask_ref, c, n_lev):
    nt_dims = (((1,), (1,)), ((), ()))
    tn_dims = (((0,), (0,)), ((), ()))
    staged = []
    for q, k, v, ball in units:
        bc = ball[0:c]
        b_last = bc[c - 1:c, :]
        levels = []
        for lev in range(n_lev):
            bm = ball[(1 + lev) * c:(2 + lev) * c]
            levels.append(((q * jnp.exp(jnp.minimum(bc - bm, 0.0))).astype(BF16),
                           (k * jnp.exp(jnp.minimum(bm - bc, 0.0))).astype(BF16)))
        staged.append(dict(qe=(q * jnp.exp(bc)).astype(BF16), kd=(k * jnp.exp(b_last - bc)).astype(BF16),
                           vb=v.astype(BF16), diag=jnp.sum(q * k, axis=-1, keepdims=True) * v,
                           b_last=b_last, levels=levels))
    atts = []
    for u in staged:
        att = None
        for lev, (ql, kl) in enumerate(u["levels"]):
            a = lax.dot_general(ql, kl, nt_dims, preferred_element_type=F32)
            a = jnp.where(mask_ref[lev] > 0.5, a, 0.0)
            att = a if att is None else att + a
        atts.append(att)
    out = []
    for u, att in zip(staged, atts):
        dk = u["kd"].shape[1]
        dv = u["vb"].shape[1]
        o_intra = _dot(att.astype(BF16), u["vb"]) + u["diag"]
        upd = lax.dot_general(u["kd"], u["vb"], tn_dims, preferred_element_type=F32)
        dec = jnp.exp(jnp.transpose(jnp.broadcast_to(u["b_last"], (dk, dk))))
        out.append((u["qe"], o_intra, upd, jnp.tile(dec, (1, dv // dk))))
    return out


def _gla_finish(o, g, gn):
    return _rms(o, gn) * (g * _sigmoid(g))


GLA_SEQ_CHUNKS = 2
GLA_STEP_SEQS = 2


def _gla_seq_kernel(q_ref, k_ref, v_ref, g_ref, la_ref, gn_ref, mall_ref, mask_ref, y_ref, sf_ref, s_scr,
                    *, c, n_lev, heads):
    t = pl.program_id(1)
    dk = q_ref.shape[1] // heads
    dv = v_ref.shape[1] // heads
    n_ch = q_ref.shape[0] // c

    @pl.when(t == 0)
    def _():
        s_scr[...] = jnp.zeros(s_scr.shape, F32)

    mall = mall_ref[...]
    units = []
    for ci in range(n_ch):
        rows = slice(ci * c, (ci + 1) * c)
        ball = _dot_sel(mall, la_ref[rows, :])
        for h in range(heads):
            ks = slice(h * dk, (h + 1) * dk)
            units.append((q_ref[rows, ks] * (dk ** -0.5), k_ref[rows, ks], v_ref[rows, h * dv:(h + 1) * dv],
                          ball[:, ks]))
    parts = _gla_intra(units, mask_ref, c, n_lev)
    state = [s_scr[h] for h in range(heads)]
    for ci in range(n_ch):
        rows = slice(ci * c, (ci + 1) * c)
        for h in range(heads):
            vs = slice(h * dv, (h + 1) * dv)
            qe, o_intra, upd, dec = parts[ci * heads + h]
            o = o_intra + _dot(qe, state[h].astype(BF16))
            state[h] = state[h] * dec + upd
            y_ref[rows, vs] = _gla_finish(o, g_ref[rows, vs], gn_ref[...]).astype(y_ref.dtype)
    for h in range(heads):
        s_scr[h] = state[h]

    @pl.when(t == pl.num_programs(1) - 1)
    def _():
        sf_ref[...] = s_scr[...]


def _gla_seq(u2, la, n_seq, seq_len, gn_g, cc, qk):
    heads = GLA_HEADS
    dk = qk // heads
    dv = cc // heads
    c = int(np.gcd(seq_len, GLA_CHUNK))
    n_ch = GLA_SEQ_CHUNKS if (seq_len // c) % GLA_SEQ_CHUNKS == 0 else 1
    tb = n_ch * c
    nt = seq_len // tb
    mall, masks, n_lev = _gla_consts(c)
    qb = 2 * cc // qk
    vb = (2 * cc + 2 * qk) // cc
    vmem = 2 * tb * (3 * qk + 3 * cc) * 4 + 3 * heads * dk * dv * 4 + 96 * n_ch * heads * c * max(dv, 128) * 4
    return pl.pallas_call(
        functools.partial(_gla_seq_kernel, c=c, n_lev=n_lev, heads=heads),
        grid=(n_seq, nt),
        in_specs=[pl.BlockSpec((tb, qk), lambda b, t: (b * nt + t, qb)),
                  pl.BlockSpec((tb, qk), lambda b, t: (b * nt + t, qb + 1)),
                  pl.BlockSpec((tb, cc), lambda b, t: (b * nt + t, vb)),
                  pl.BlockSpec((tb, cc), lambda b, t: (b * nt + t, vb + 1)),
                  pl.BlockSpec((tb, qk), lambda b, t: (b * nt + t, 0)),
                  pl.BlockSpec((1, dv), lambda b, t: (0, 0)),
                  pl.BlockSpec(mall.shape, lambda b, t: (0, 0)),
                  pl.BlockSpec(masks.shape, lambda b, t: (0, 0, 0))],
        out_specs=[pl.BlockSpec((tb, cc), lambda b, t: (b * nt + t, 0)),
                   pl.BlockSpec((None, None, heads, dk, dv), lambda b, t: (0, b, 0, 0, 0))],
        out_shape=[jax.ShapeDtypeStruct((n_seq * seq_len, cc), BF16),
                   jax.ShapeDtypeStruct((1, n_seq, heads, dk, dv), F32)],
        scratch_shapes=[pltpu.VMEM((heads, dk, dv), F32)],
        compiler_params=_params(("arbitrary", "arbitrary"), vmem),
        name="gla_seq",
    )(u2, u2, u2, u2, la, gn_g.reshape(1, dv), mall, masks)


def _gla_step_kernel(q_ref, k_ref, v_ref, g_ref, la_ref, st_ref, gn_ref, mall_ref, mask_ref, y_ref, sn_ref,
                     *, c, n_lev, heads):
    dk = q_ref.shape[2] // heads
    dv = v_ref.shape[2] // heads
    mall = mall_ref[...]

    n_un = GLA_STEP_SEQS if q_ref.shape[0] % GLA_STEP_SEQS == 0 else 1

    def body(t, carry):
        units = []
        for ss in range(n_un):
            s = t * n_un + ss
            ball = _dot_sel(mall, la_ref[s])
            for h in range(heads):
                ks = slice(h * dk, (h + 1) * dk)
                units.append((q_ref[s, :, ks] * (dk ** -0.5), k_ref[s, :, ks], v_ref[s, :, h * dv:(h + 1) * dv],
                              ball[:, ks]))
        parts = _gla_intra(units, mask_ref, c, n_lev)
        for ss in range(n_un):
            s = t * n_un + ss
            for h in range(heads):
                vs = slice(h * dv, (h + 1) * dv)
                qe, o_intra, upd, dec = parts[ss * heads + h]
                s0 = st_ref[s, h]
                o = o_intra + _dot(qe, s0.astype(BF16))
                sn_ref[s, h] = s0 * dec + upd
                y_ref[s, :, vs] = _gla_finish(o, g_ref[s, :, vs], gn_ref[...])
        return carry

    lax.fori_loop(0, q_ref.shape[0] // n_un, body, 0)


def _gla_step(u3, la3, state, gn_g, cc, qk):
    n_seq, t_len, _ = u3.shape
    heads = GLA_HEADS
    dk = qk // heads
    dv = cc // heads
    assert t_len <= GLA_CHUNK and GLA_CHUNK % t_len == 0, "single-chunk path"
    mall, masks, n_lev = _gla_consts(t_len)
    bs = min(8, n_seq)
    qb = 2 * cc // qk
    vb = (2 * cc + 2 * qk) // cc
    vmem = 2 * bs * t_len * (3 * qk + 3 * cc) * 4 + 4 * bs * heads * dk * dv * 4 + 64 * 8 * max(dv, 128) * 4
    return pl.pallas_call(
        functools.partial(_gla_step_kernel, c=t_len, n_lev=n_lev, heads=heads),
        grid=(n_seq // bs,),
        in_specs=[pl.BlockSpec((bs, t_len, qk), lambda i: (i, 0, qb)),
                  pl.BlockSpec((bs, t_len, qk), lambda i: (i, 0, qb + 1)),
                  pl.BlockSpec((bs, t_len, cc), lambda i: (i, 0, vb)),
                  pl.BlockSpec((bs, t_len, cc), lambda i: (i, 0, vb + 1)),
                  pl.BlockSpec((bs, t_len, qk), lambda i: (i, 0, 0)),
                  pl.BlockSpec((None, bs, heads, dk, dv), lambda i: (0, i, 0, 0, 0)),
                  pl.BlockSpec((1, dv), lambda i: (0, 0)),
                  pl.BlockSpec(mall.shape, lambda i: (0, 0)),
                  pl.BlockSpec(masks.shape, lambda i: (0, 0, 0))],
        out_specs=[pl.BlockSpec((bs, t_len, cc), lambda i: (i, 0, 0)),
                   pl.BlockSpec((None, bs, heads, dk, dv), lambda i: (0, i, 0, 0, 0))],
        out_shape=[jax.ShapeDtypeStruct((n_seq, t_len, cc), F32),
                   jax.ShapeDtypeStruct((1, n_seq, heads, dk, dv), F32)],
        compiler_params=_params(("arbitrary",), vmem),
        name="gla_step",
    )(u3, u3, u3, u3, la3, state, gn_g.reshape(1, dv), mall, masks)


def _out_kernel(x_ref, yc_ref, yg_ref, wo_ref, g1_ref, sh_ref, sc_ref, n2_ref, wr_ref, br_ref,
                x1_ref, h2_ref, ti_ref, tw_ref, *, tm, seq_len):
    i = pl.program_id(0)
    cc = yc_ref.shape[1]
    n_exp = wr_ref.shape[1]
    mix = _dot(yc_ref[...].astype(BF16), wo_ref[0:cc, :]) + _dot(yg_ref[...].astype(BF16), wo_ref[cc:, :])
    x1 = x_ref[...] + _expand_rows(g1_ref, i, tm, seq_len) * mix
    x1_ref[...] = x1
    h2 = _rms(x1, n2_ref[...]) * (1.0 + _expand_rows(sc_ref, i, tm, seq_len)) + _expand_rows(sh_ref, i, tm, seq_len)
    h2_ref[...] = h2

    logits = _dot(h2.astype(BF16), wr_ref[...].astype(BF16)) + br_ref[...]
    lane = lax.broadcasted_iota(I32, logits.shape, 1)
    vals = []
    for r in range(TOP_K):
        m = jnp.max(logits, axis=-1, keepdims=True)
        idx = jnp.min(jnp.where(logits == m, lane, n_exp), axis=-1, keepdims=True)
        vals.append(m)
        ti_ref[:, r:r + 1] = idx
        logits = jnp.where(lane == idx, -jnp.inf, logits)
    ex = [jnp.exp(v - vals[0]) for v in vals]
    den = ex[0] + ex[1] + ex[2] + ex[3]
    for r in range(TOP_K):
        tw_ref[:, r:r + 1] = ex[r] / den


def _out_proj(x2, yc, yg, wo_bf, g1, sh2, sc2, n2, w_router, b_router, seq_len):
    n, d = x2.shape
    cc = yc.shape[1]
    n_seq = g1.shape[0]
    n_exp = w_router.shape[1]
    tm = _tile(n, 256)
    vmem = (6 * tm * d * 4 + 4 * tm * cc * 4 + 2 * 2 * cc * d * 2 + 8 * tm * d * 4)
    tok = lambda i: (i, 0)
    fix = lambda i: (0, 0)
    return pl.pallas_call(
        functools.partial(_out_kernel, tm=tm, seq_len=seq_len),
        grid=(n // tm,),
        in_specs=[pl.BlockSpec((tm, d), tok),
                  pl.BlockSpec((tm, cc), tok),
                  pl.BlockSpec((tm, cc), tok),
                  pl.BlockSpec((2 * cc, d), fix),
                  _table_spec(n_seq, d, tm, seq_len),
                  _table_spec(n_seq, d, tm, seq_len),
                  _table_spec(n_seq, d, tm, seq_len),
                  pl.BlockSpec((1, d), fix),
                  pl.BlockSpec((d, n_exp), fix),
                  pl.BlockSpec((1, n_exp), fix)],
        out_specs=[pl.BlockSpec((tm, d), tok),
                   pl.BlockSpec((tm, d), tok),
                   pl.BlockSpec((tm, TOP_K), tok),
                   pl.BlockSpec((tm, TOP_K), tok)],
        out_shape=[jax.ShapeDtypeStruct((n, d), F32),
                   jax.ShapeDtypeStruct((n, d), F32),
                   jax.ShapeDtypeStruct((n, TOP_K), I32),
                   jax.ShapeDtypeStruct((n, TOP_K), F32)],
        compiler_params=_params(("arbitrary",), vmem),
        name="out_proj",
    )(x2, yc, yg, wo_bf, g1, sh2, sc2, n2.reshape(1, d), w_router, b_router.reshape(1, n_exp))


def _rank_kernel(ti_ref, slot_ref, cnt_ref, off_ref, carry, offs, *, tr, n_exp, group):
    p = pl.program_id(0)
    i = pl.program_id(1)
    lane = lax.broadcasted_iota(I32, (tr, n_exp), 1)
    ti = ti_ref[...]
    hot = jnp.zeros((tr, n_exp), F32)
    for k in range(TOP_K):
        hot = hot + (lane == ti[:, k:k + 1]).astype(F32)

    @pl.when((p == 0) & (i == 0))
    def _():
        carry[...] = jnp.zeros(carry.shape, F32)

    @pl.when((p == 1) & (i == 0))
    def _():
        cnt = carry[0:1, :]
        padded = jnp.ceil(cnt * (1.0 / group)) * group
        r = lax.broadcasted_iota(I32, (n_exp, n_exp), 0)
        c = lax.broadcasted_iota(I32, (n_exp, n_exp), 1)
        before = (r < c).astype(BF16)
        hi, mid, lo = _split3(jnp.broadcast_to(padded, (8, n_exp)))
        off = _dot(hi, before) + _dot(mid, before) + _dot(lo, before)
        offs[...] = off
        cnt_ref[...] = cnt.astype(I32)
        off_ref[...] = off[0:1, :].astype(I32)
        carry[...] = jnp.zeros(carry.shape, F32)

    @pl.when(p == 1)
    def _():
        r = lax.broadcasted_iota(I32, (tr, tr), 0)
        c = lax.broadcasted_iota(I32, (tr, tr), 1)
        earlier = (c < r).astype(BF16)
        tot = _dot(earlier, hot.astype(BF16)) + carry[0:1, :] + offs[0:1, :]
        for k in range(TOP_K):
            sel = jnp.where(lane == ti[:, k:k + 1], tot, 0.0)
            slot_ref[:, k:k + 1] = jnp.sum(sel, axis=-1, keepdims=True).astype(I32)

    carry[...] = carry[...] + jnp.sum(hot, axis=0, keepdims=True)


def _rank(top_i, n_exp, group):
    n = top_i.shape[0]
    tr = _tile(n, 256)
    return pl.pallas_call(
        functools.partial(_rank_kernel, tr=tr, n_exp=n_exp, group=group),
        grid=(2, n // tr),
        in_specs=[pl.BlockSpec((tr, TOP_K), lambda p, i: (i, 0))],
        out_specs=[pl.BlockSpec((tr, TOP_K), lambda p, i: (i * p, 0)),
                   pl.BlockSpec((1, n_exp), lambda p, i: (0, 0)),
                   pl.BlockSpec((1, n_exp), lambda p, i: (0, 0))],
        out_shape=[jax.ShapeDtypeStruct((n, TOP_K), I32),
                   jax.ShapeDtypeStruct((1, n_exp), I32),
                   jax.ShapeDtypeStruct((1, n_exp), I32)],
        scratch_shapes=[pltpu.VMEM((8, n_exp), F32), pltpu.VMEM((8, n_exp), F32)],
        compiler_params=_params(("arbitrary", "arbitrary"), 8 * tr * tr * 4),
        name="rank",
    )(top_i)


def _scatter_rows(slot_ref, h_ref, xs_ref, sem, td):
    def issue(r, carry):
        for k in range(TOP_K):
            s = slot_ref[0, r * TOP_K + k]
            pltpu.make_async_copy(h_ref.at[pl.ds(r, 1), :], xs_ref.at[pl.ds(s, 1), :], sem).start()
        return carry

    lax.fori_loop(0, td, issue, 0)
    for k in range(TOP_K):
        pltpu.make_async_copy(h_ref, xs_ref.at[pl.ds(0, td), :], sem).wait()


def _dispatch_kernel(cnt, off, slot_ref, ha_ref, hb_ref, xs_ref, zbuf, sem, zsem, *, td, group, n_a):
    n_exp = cnt.shape[0]
    n_rows = xs_ref.shape[0]
    i = pl.program_id(0)

    def padded(e):
        return (cnt[e] + group - 1) // group * group

    def z_row(row):
        return pltpu.make_async_copy(zbuf.at[pl.ds(0, 1), :], xs_ref.at[pl.ds(row, 1), :], zsem)

    def z_block(row):
        return pltpu.make_async_copy(zbuf, xs_ref.at[pl.ds(pl.multiple_of(row, group), group), :], zsem)

    def for_pad_rows(fn):
        def expert(e, carry):
            lax.fori_loop(off[e] + cnt[e], off[e] + padded(e), lambda r, c: (fn(r), c)[1], 0)
            return carry
        lax.fori_loop(0, n_exp, expert, 0)

    def for_tail_blocks(fn):
        first = (off[n_exp - 1] + padded(n_exp - 1)) // group
        lax.fori_loop(first, n_rows // group, lambda b, c: (fn(b * group), c)[1], 0)

    @pl.when(i == 0)
    def _():
        zbuf[...] = jnp.zeros(zbuf.shape, zbuf.dtype)
        for_pad_rows(lambda r: z_row(r).start())
        for_tail_blocks(lambda r: z_block(r).start())

    @pl.when(i < n_a)
    def _():
        _scatter_rows(slot_ref, ha_ref, xs_ref, sem, td)

    @pl.when(i >= n_a)
    def _():
        _scatter_rows(slot_ref, hb_ref, xs_ref, sem, td)

    @pl.when(i == pl.num_programs(0) - 1)
    def _():
        for_pad_rows(lambda r: z_row(0).wait())
        for_tail_blocks(lambda r: z_block(0).wait())


def _dispatch(slots, h_a, h_b, cnt, off, n_rows, group):
    d = h_a.shape[1]
    td = _tile(int(np.gcd(h_a.shape[0], h_b.shape[0])), 256)
    n_a, n_b = h_a.shape[0] // td, h_b.shape[0] // td
    grid_spec = pltpu.PrefetchScalarGridSpec(
        num_scalar_prefetch=2,
        grid=(n_a + n_b,),
        in_specs=[pl.BlockSpec((None, 1, td * TOP_K), lambda i, *_: (i, 0, 0), memory_space=pltpu.SMEM),
                  pl.BlockSpec((td, d), lambda i, *_: (jnp.minimum(i, n_a - 1), 0)),
                  pl.BlockSpec((td, d), lambda i, *_: (jnp.maximum(i - n_a, 0), 0))],
        out_specs=pl.BlockSpec(memory_space=pl.ANY),
        scratch_shapes=[pltpu.VMEM((group, d), h_a.dtype), pltpu.SemaphoreType.DMA, pltpu.SemaphoreType.DMA])
    return pl.pallas_call(
        functools.partial(_dispatch_kernel, td=td, group=group, n_a=n_a),
        grid_spec=grid_spec,
        out_shape=jax.ShapeDtypeStruct((n_rows, d), h_a.dtype),
        compiler_params=_params(("arbitrary",), 4 * td * d * 4 + group * d * 4),
        name="dispatch",
    )(cnt, off, slots.reshape(n_a + n_b, 1, td * TOP_K), h_a, h_b)


MOE_CHUNK = 2048
MOE_SUB = 128
MOE_MM = 2
MOE_TF = 256
MOE_TN = 256


def _for_row_blocks(n_units, body):
    n = n_units // MOE_MM

    def block(b):
        body(b * MOE_MM, MOE_MM)

    def trip(t, carry):
        for u in range(4):
            block(t * 4 + u)
        return carry

    lax.fori_loop(0, n // 4, trip, 0)

    @pl.when((n // 2) % 2 == 1)
    def _():
        block((n // 4) * 4)
        block((n // 4) * 4 + 1)

    @pl.when(n % 2 == 1)
    def _():
        block(n - 1)

    @pl.when(n_units % MOE_MM == 1)
    def _():
        body(n_units - 1, 1)


def _moe_kernel(che, chrow, chn, tail, xs_ref, wg_ref, wu_ref, bg_ref, bu_ref, wd_ref, bd_ref, yb_ref,
                xbuf, act, wgb, wub, wdb, obuf, semx, semo, *, n_f, n_n, n_chunks):
    c = pl.program_id(0)
    j = pl.program_id(1)
    nsb = chn[c]
    sub = MOE_SUB

    def x_copy(cc, i):
        row = pl.multiple_of(chrow[cc] + i * sub, sub)
        return pltpu.make_async_copy(xs_ref.at[pl.ds(row, sub), :],
                                     xbuf.at[pl.ds(pl.multiple_of(i * sub, sub), sub), :], semx)

    def x_fetch(cc):
        lax.fori_loop(0, chn[cc], lambda i, carry: (x_copy(cc, i).start(), carry)[1], 0)

    def o_copy(slot, i, jj):
        row = pl.multiple_of(chrow[c] + i * sub, sub)
        col = pl.multiple_of(jj * MOE_TN, MOE_TN)
        return pltpu.make_async_copy(obuf.at[slot, pl.ds(pl.multiple_of(i * sub, sub), sub), :],
                                     yb_ref.at[pl.ds(row, sub), pl.ds(col, MOE_TN)], semo.at[slot])

    @pl.when((c == 0) & (j == 0))
    def _():
        x_fetch(0)

    @pl.when(j == 0)
    def _():
        lax.fori_loop(0, nsb, lambda i, carry: (x_copy(c, 0).wait(), carry)[1], 0)

    @pl.when((j == n_f) & (c + 1 < n_chunks))
    def _():
        x_fetch(jnp.minimum(c + 1, n_chunks - 1))

    @pl.when((j < n_f) & (nsb > 0))
    def _():
        wgb[...] = wg_ref[...].astype(BF16)
        wub[...] = wu_ref[...].astype(BF16)

        def row_block(u, n):
            rows = pl.ds(pl.multiple_of(u * sub, sub), n * sub)
            x = xbuf[rows, :].astype(BF16)
            gate = jnp.minimum(_dot(x, wgb[...]) + bg_ref[...], SWIGLU_LIMIT)
            up = jnp.clip(_dot(x, wub[...]) + bu_ref[...], -SWIGLU_LIMIT, SWIGLU_LIMIT)
            glu = gate * _sigmoid(SWIGLU_ALPHA * gate)
            act[j, rows, :] = ((up + 1.0) * glu).astype(BF16)

        _for_row_blocks(nsb, row_block)

    @pl.when((j >= n_f) & (nsb > 0))
    def _():
        jj = j - n_f
        slot = jj % 2
        wdb[...] = wd_ref[...].astype(BF16)

        def retire(s):
            def one(i, carry):
                o_copy(s, 0, 0).wait()
                return carry
            lax.fori_loop(0, nsb, one, 0)

        @pl.when(jj >= 2)
        def _():
            retire(slot)

        def row_block(u, n):
            rows = pl.ds(pl.multiple_of(u * sub, sub), n * sub)
            a = jnp.concatenate([act[f, rows, :] for f in range(n_f)], axis=1)
            obuf[slot, rows, :] = _dot(a, wdb[...]) + bd_ref[...]
            for k in range(n):
                o_copy(slot, u + k, jj).start()

        _for_row_blocks(nsb, row_block)

        @pl.when(jj == n_n - 1)
        def _():
            retire(slot)
            if n_n >= 2:
                retire(1 - slot)

    @pl.when((c == n_chunks - 1) & (j == n_f + n_n - 1))
    def _():
        n_tail = (yb_ref.shape[0] - tail[0]) // sub
        obuf[0, 0:sub, :] = jnp.zeros((sub, MOE_TN), F32)

        def z_copy(i, js):
            row = pl.multiple_of(tail[0] + i * sub, sub)
            return pltpu.make_async_copy(obuf.at[0, pl.ds(0, sub), :],
                                         yb_ref.at[pl.ds(row, sub), pl.ds(js * MOE_TN, MOE_TN)], semo.at[0])

        def fill(i, carry):
            for js in range(n_n):
                z_copy(i, js).start()
            return carry

        def drain(i, carry):
            for js in range(n_n):
                z_copy(0, js).wait()
            return carry

        lax.fori_loop(0, n_tail, fill, 0)
        lax.fori_loop(0, n_tail, drain, 0)


def _moe(xs, yb_rows, ch_e, ch_row, ch_n, tail, w_gate, b_gate, w_up, b_up, w_down, b_down):
    n_exp, d, f = w_gate.shape
    n_f = f // MOE_TF
    n_n = d // MOE_TN
    n_chunks = ch_e.shape[0]

    def w1(c, j, che, chrow, chn, tail):
        return (che[c], 0, jnp.where(chn[c] > 0, jnp.minimum(j, n_f - 1), n_f - 1))

    def w2(c, j, che, chrow, chn, tail):
        return (che[c], 0, jnp.where(chn[c] > 0, jnp.maximum(j - n_f, 0), n_n - 1))

    vmem = (MOE_CHUNK * d * 4 + MOE_CHUNK * f * 2
            + 2 * 2 * d * MOE_TF * 4 + 2 * f * MOE_TN * 4 + 2 * d * MOE_TF * 2 + f * MOE_TN * 2
            + 2 * MOE_CHUNK * MOE_TN * 4 + 8 * MOE_MM * MOE_SUB * max(d, f) * 4)
    grid_spec = pltpu.PrefetchScalarGridSpec(
        num_scalar_prefetch=4,
        grid=(n_chunks, n_f + n_n),
        in_specs=[pl.BlockSpec(memory_space=pl.ANY),
                  pl.BlockSpec((None, d, MOE_TF), w1),
                  pl.BlockSpec((None, d, MOE_TF), w1),
                  pl.BlockSpec((None, 1, MOE_TF), w1),
                  pl.BlockSpec((None, 1, MOE_TF), w1),
                  pl.BlockSpec((None, f, MOE_TN), w2),
                  pl.BlockSpec((None, 1, MOE_TN), w2)],
        out_specs=pl.BlockSpec(memory_space=pl.ANY),
        scratch_shapes=[pltpu.VMEM((MOE_CHUNK, d), F32),
                        pltpu.VMEM((n_f, MOE_CHUNK, MOE_TF), BF16),
                        pltpu.VMEM((d, MOE_TF), BF16),
                        pltpu.VMEM((d, MOE_TF), BF16),
                        pltpu.VMEM((f, MOE_TN), BF16),
                        pltpu.VMEM((2, MOE_CHUNK, MOE_TN), F32),
                        pltpu.SemaphoreType.DMA,
                        pltpu.SemaphoreType.DMA((2,))])
    return pl.pallas_call(
        functools.partial(_moe_kernel, n_f=n_f, n_n=n_n, n_chunks=n_chunks),
        grid_spec=grid_spec,
        out_shape=jax.ShapeDtypeStruct((yb_rows, d), F32),
        compiler_params=_params(("arbitrary", "arbitrary"), vmem),
        name="moe",
    )(ch_e, ch_row, ch_n, tail, xs, w_gate, w_up, b_gate.reshape(n_exp, 1, f), b_up.reshape(n_exp, 1, f),
      w_down, b_down.reshape(n_exp, 1, d))


def _chunk_table(cnt, off, n_chunks):
    n_exp = cnt.shape[0]
    per = MOE_CHUNK // MOE_SUB
    nsb = (cnt + MOE_SUB - 1) // MOE_SUB
    nch = (nsb + per - 1) // per
    end = jnp.cumsum(nch)
    start = end - nch
    cidx = jnp.arange(n_chunks, dtype=I32)
    owner = lambda ci: jnp.minimum(jnp.sum((end[None, :] <= ci[:, None]).astype(I32), axis=1), n_exp - 1)
    e = owner(cidx)
    used = cidx < end[-1]
    last_e = owner(end[-1:] - 1)[0]
    local = cidx - start[e]
    ch_e = jnp.where(used, e, last_e)
    ch_row = jnp.where(used, off[e] + local * MOE_CHUNK, 0).astype(I32)
    ch_n = jnp.where(used, jnp.minimum(per, nsb[e] - local * per), 0).astype(I32)
    tail = (off[-1] + nsb[-1] * MOE_SUB).astype(I32).reshape(1)
    return ch_e, ch_row, ch_n, tail


def _combine_kernel(slot_ref, next_slot_ref, tw_ref, x1_ref, g2_ref, gf_ref, yb_ref, y_ref, gbuf, sem,
                    *, tc, seq_len):
    i = pl.program_id(0)
    cur = i % 2

    def gather(s_ref, half):
        def rows(r, carry):
            for k in range(TOP_K):
                s = s_ref[0, r * TOP_K + k]
                pltpu.make_async_copy(yb_ref.at[pl.ds(s, 1), :], gbuf.at[half, k, pl.ds(r, 1), :],
                                      sem.at[half]).start()
            return carry
        lax.fori_loop(0, tc, rows, 0)

    @pl.when(i == 0)
    def _():
        gather(slot_ref, 0)

    @pl.when(i + 1 < pl.num_programs(0))
    def _():
        gather(next_slot_ref, 1 - cur)

    for k in range(TOP_K):
        pltpu.make_async_copy(yb_ref.at[pl.ds(0, tc), :], gbuf.at[cur, k], sem.at[cur]).wait()
    tw = tw_ref[...]
    y = tw[:, 0:1] * gbuf[cur, 0]
    for k in range(1, TOP_K):
        y = y + tw[:, k:k + 1] * gbuf[cur, k]
    x2 = x1_ref[...] + _expand_rows(g2_ref, i, tc, seq_len) * y
    y_ref[...] = _rms(x2, gf_ref[...])


def _combine(slots, top_w, x1, g2, gf, yb, seq_len):
    n, d = x1.shape
    n_seq = g2.shape[0]
    tc = _tile(n, 128)
    nt = n // tc
    vmem = 2 * TOP_K * tc * d * 4 + 4 * tc * d * 4 + 6 * tc * d * 4
    slots3 = slots.reshape(nt, 1, tc * TOP_K)
    return pl.pallas_call(
        functools.partial(_combine_kernel, tc=tc, seq_len=seq_len),
        grid=(nt,),
        in_specs=[pl.BlockSpec((None, 1, tc * TOP_K), lambda i: (i, 0, 0), memory_space=pltpu.SMEM),
                  pl.BlockSpec((None, 1, tc * TOP_K), lambda i: (jnp.minimum(i + 1, nt - 1), 0, 0),
                               memory_space=pltpu.SMEM),
                  pl.BlockSpec((tc, TOP_K), lambda i: (i, 0)),
                  pl.BlockSpec((tc, d), lambda i: (i, 0)),
                  _table_spec(n_seq, d, tc, seq_len),
                  pl.BlockSpec((1, d), lambda i: (0, 0)),
                  pl.BlockSpec(memory_space=pl.ANY)],
        out_specs=pl.BlockSpec((tc, d), lambda i: (i, 0)),
        out_shape=jax.ShapeDtypeStruct((n, d), F32),
        scratch_shapes=[pltpu.VMEM((2, TOP_K, tc, d), F32), pltpu.SemaphoreType.DMA((2,))],
        compiler_params=_params(("arbitrary",), vmem),
        name="combine",
    )(slots3, slots3, top_w, x1, g2, gf.reshape(1, d), yb)


def _layer(xp, xs, cp, cs, conv_state, gla_state, w):
    (w_ada, b_ada, norm1_g, w_in, w_dw, b_dw, conv_ln_g, conv_ln_b, w_alpha, b_alpha, gla_norm_g, w_out,
     norm2_g, w_router, b_router, w_gate, b_gate, w_up, b_up, w_down, b_down, final_g) = w
    bp, tp, d = xp.shape
    bs, ts, _ = xs.shape
    cc = w_dw.shape[1]
    qk = w_alpha.shape[1]
    n_exp = w_router.shape[1]
    n_main = 2 * cc + 2 * qk + 2 * cc
    np_, ns_ = bp * tp, bs * ts

    mod = _ada(jnp.concatenate([cp, cs], axis=0), w_ada, b_ada)
    mods_p = [mod[:bp, i * d:(i + 1) * d] for i in range(6)]
    mods_s = [mod[bp:, i * d:(i + 1) * d] for i in range(6)]
    w_in_t = w_in.T
    w_low = w_in_t[n_main:]
    w_main_bf = w_in_t[:n_main].astype(BF16)
    wo_bf = w_out.astype(BF16)

    xp2 = xp.reshape(np_, d)
    xs2 = xs.reshape(ns_, d)

    u_p, la_p = _in_proj(xp2, mods_p[0], mods_p[1], norm1_g, w_main_bf, w_low, w_alpha, b_alpha, tp)
    yc_p, conv_p = _conv_seq(u_p, bp, tp, w_dw, b_dw, conv_ln_g, conv_ln_b)
    yg_p, gla_p = _gla_seq(u_p, la_p, bp, tp, gla_norm_g, cc, qk)
    x1_p, h2_p, ti_p, tw_p = _out_proj(xp2, yc_p, yg_p, wo_bf, mods_p[2], mods_p[3], mods_p[4], norm2_g,
                                       w_router, b_router, tp)

    u_s, la_s = _in_proj(xs2, mods_s[0], mods_s[1], norm1_g, w_main_bf, w_low, w_alpha, b_alpha, ts)
    yc_s, conv_s = _conv_step(u_s, jnp.transpose(conv_state, (0, 2, 1, 3)), ts, w_dw, b_dw, conv_ln_g, conv_ln_b)
    conv_s = jnp.transpose(conv_s, (0, 2, 1, 3))
    yg_s, gla_s = _gla_step(u_s.reshape(bs, ts, n_main), la_s.reshape(bs, ts, qk), gla_state, gla_norm_g, cc, qk)
    x1_s, h2_s, ti_s, tw_s = _out_proj(xs2, yc_s, yg_s.reshape(ns_, cc), wo_bf, mods_s[2],
                                       mods_s[3], mods_s[4], norm2_g, w_router, b_router, ts)

    n_asg = (np_ + ns_) * TOP_K
    slots, cnt, off = _rank(jnp.concatenate([ti_p, ti_s], axis=0), n_exp, MOE_SUB)
    rows = -(-(n_asg + n_exp * MOE_SUB) // MOE_SUB) * MOE_SUB
    xsort = _dispatch(slots, h2_p, h2_s, cnt[0], off[0], rows, MOE_SUB)
    n_chunks = (n_asg + n_exp * MOE_SUB) // MOE_CHUNK + n_exp
    ch_e, ch_row, ch_n, tail = _chunk_table(cnt[0], off[0], n_chunks)
    yb = _moe(xsort, rows, ch_e, ch_row, ch_n, tail, w_gate, b_gate, w_up, b_up, w_down, b_down)

    y_p = _combine(slots[:np_], tw_p, x1_p, mods_p[5], final_g, yb, tp)
    y_s = _combine(slots[np_:], tw_s, x1_s, mods_s[5], final_g, yb, ts)
    return y_p.reshape(bp, tp, d), y_s.reshape(bs, ts, d), conv_p, gla_p, conv_s, gla_s


def kernel(x_prompt, x_sample, c_prompt, c_sample, state_conv, state_gla, w_ada, b_ada, norm1_g, w_in, w_dw, b_dw,
           conv_ln_g, conv_ln_b, w_alpha, b_alpha, gla_norm_g, w_out, norm2_g, w_router, b_router, w_gate, b_gate,
           w_up, b_up, w_down, b_down, final_norm_g):
    assert w_ada.shape[0] == 1, "the final norm is fused into the (single) layer"
    first = lambda a: a.reshape(a.shape[1:])
    w = tuple(first(a) for a in (w_ada, b_ada, norm1_g, w_in, w_dw, b_dw, conv_ln_g, conv_ln_b, w_alpha, b_alpha,
                                 gla_norm_g, w_out, norm2_g, w_router, b_router, w_gate, b_gate, w_up, b_up,
                                 w_down, b_down)) + (final_norm_g,)
    return _layer(x_prompt, x_sample, c_prompt, c_sample, state_conv, state_gla, w)
```

```python
import functools

import numpy as np
import jax
import jax.numpy as jnp
from jax import lax
from jax.experimental import pallas as pl
from jax.experimental.pallas import tpu as pltpu

F32 = jnp.float32
BF16 = jnp.bfloat16
I32 = jnp.int32

EPS = 1e-5
GLA_HEADS = 4
GLA_GATE_NORM = 16.0
GLA_CHUNK = 64
TOP_K = 4
SWIGLU_LIMIT = 7.0
SWIGLU_ALPHA = 1.702

V7X_LANES = 128
V7X_VMEM_BYTES = 64 * 1024 * 1024
VMEM_CAP_BYTES = V7X_VMEM_BYTES - 6 * 1024 * 1024


def _params(semantics, vmem_bytes):
    limit = int(min(VMEM_CAP_BYTES, max(vmem_bytes * 5 // 4 + (4 << 20), 16 << 20)))
    return pltpu.CompilerParams(dimension_semantics=semantics, vmem_limit_bytes=limit)


def _tile(n, want):
    t = min(want, n)
    while n % t or t % 8:
        t -= 8 if t % 8 == 0 else t % 8
        assert t > 0, (n, want)
    return t


NT_DIMS = (((1,), (1,)), ((), ()))


def _dot(a, b):
    return jnp.dot(a, b, preferred_element_type=F32)


def _split3(v):
    hi = v.astype(BF16)
    r = v - hi.astype(F32)
    mid = r.astype(BF16)
    lo = (r - mid.astype(F32)).astype(BF16)
    return hi, mid, lo


def _dot_sel(m01, v):
    hi, mid, lo = _split3(v)
    return _dot(m01, hi) + _dot(m01, mid) + _dot(m01, lo)


def _sigmoid(x):
    return jax.nn.sigmoid(x)


def _rms(x, g):
    return x * lax.rsqrt(jnp.mean(x * x, axis=-1, keepdims=True) + EPS) * g


def _expand_rows(m_ref, i, tm, seq_len):
    if seq_len >= tm:
        return m_ref[pl.ds((i * tm) // seq_len, 1), :]
    nseq = tm // seq_len
    r = lax.broadcasted_iota(I32, (tm, nseq), 0)
    c = lax.broadcasted_iota(I32, (tm, nseq), 1) * seq_len
    sel = ((r >= c) & (r < c + seq_len)).astype(BF16)
    return _dot_sel(sel, m_ref[...])


def _table_spec(n_seq, d, tm, seq_len):
    if seq_len >= tm:
        return pl.BlockSpec((n_seq, d), lambda i, *_: (0, 0))
    return pl.BlockSpec((tm // seq_len, d), lambda i, *_: (i, 0))


def _ada_kernel(c_ref, w_ref, b_ref, o_ref):
    c = c_ref[...]
    s = (c * _sigmoid(c)).astype(BF16)
    o_ref[...] = _dot(s, w_ref[...].astype(BF16)) + b_ref[...]


def _ada(c_all, w_ada, b_ada):
    bc, d = c_all.shape
    n6 = w_ada.shape[1]
    tn = min(1024, n6)
    return pl.pallas_call(
        _ada_kernel,
        grid=(n6 // tn,),
        in_specs=[pl.BlockSpec((bc, d), lambda j: (0, 0)),
                  pl.BlockSpec((d, tn), lambda j: (0, j)),
                  pl.BlockSpec((1, tn), lambda j: (0, j))],
        out_specs=pl.BlockSpec((bc, tn), lambda j: (0, j)),
        out_shape=jax.ShapeDtypeStruct((bc, n6), F32),
        compiler_params=_params(("arbitrary",), 2 * d * tn * 4 + d * tn * 2 + 4 * bc * (d + tn) * 4),
        name="ada",
    )(c_all, w_ada, b_ada.reshape(1, n6))


IN_TN = 512


def _in_kernel(x_ref, sh_ref, sc_ref, g_ref, w_hbm, wlow_ref, wal_ref, bal_ref, u_ref, la_ref, w_vmem, sem,
               *, tm, seq_len):
    i = pl.program_id(0)
    load_w = pltpu.make_async_copy(w_hbm, w_vmem, sem)

    @pl.when(i == 0)
    def _():
        load_w.start()

    xn = _rms(x_ref[...], g_ref[...])
    h = xn * (1.0 + _expand_rows(sc_ref, i, tm, seq_len)) + _expand_rows(sh_ref, i, tm, seq_len)
    hb = h.astype(BF16)
    a_low = lax.dot_general(hb, wlow_ref[...].astype(BF16), NT_DIMS, preferred_element_type=F32)
    z = _dot(a_low.astype(BF16), wal_ref[...].astype(BF16)) + bal_ref[...]
    la_ref[...] = (jnp.minimum(z, 0.0) - jnp.log(1.0 + jnp.exp(-jnp.abs(z)))) * (1.0 / GLA_GATE_NORM)

    @pl.when(i == 0)
    def _():
        load_w.wait()

    for jn in range(u_ref.shape[1] // IN_TN):
        cols = slice(jn * IN_TN, (jn + 1) * IN_TN)
        u_ref[:, cols] = lax.dot_general(hb, w_vmem[cols, :], NT_DIMS, preferred_element_type=F32)


def _in_proj(x2, sh, sc, g, w_main_bf, w_low, w_alpha, b_alpha, seq_len):
    n, d = x2.shape
    n_main = w_main_bf.shape[0]
    n_seq = sh.shape[0]
    qk = w_alpha.shape[1]
    rank = w_low.shape[0]
    tm = _tile(n, 256)
    vmem = d * n_main * 2 + 2 * tm * d * 4 + 2 * tm * n_main * 4 + 2 * tm * qk * 4 + 4 * tm * d * 4
    fix = lambda i: (0, 0)
    return pl.pallas_call(
        functools.partial(_in_kernel, tm=tm, seq_len=seq_len),
        grid=(n // tm,),
        in_specs=[pl.BlockSpec((tm, d), lambda i: (i, 0)),
                  _table_spec(n_seq, d, tm, seq_len),
                  _table_spec(n_seq, d, tm, seq_len),
                  pl.BlockSpec((1, d), fix),
                  pl.BlockSpec(memory_space=pl.ANY),
                  pl.BlockSpec((rank, d), fix),
                  pl.BlockSpec((rank, qk), fix),
                  pl.BlockSpec((1, qk), fix)],
        out_specs=[pl.BlockSpec((tm, n_main), lambda i: (i, 0)),
                   pl.BlockSpec((tm, qk), lambda i: (i, 0))],
        out_shape=[jax.ShapeDtypeStruct((n, n_main), F32), jax.ShapeDtypeStruct((n, qk), F32)],
        scratch_shapes=[pltpu.VMEM((n_main, d), BF16), pltpu.SemaphoreType.DMA],
        compiler_params=_params(("arbitrary",), vmem),
        name="in_proj",
    )(x2, sh, sc, g.reshape(1, d), w_main_bf, w_low, w_alpha, b_alpha.reshape(1, qk))


CONV_HALO = 32


def _ln_silu(y, g, b):
    mu = jnp.mean(y, axis=-1, keepdims=True)
    yc = y - mu
    var = jnp.mean(yc * yc, axis=-1, keepdims=True)
    yn = yc * lax.rsqrt(var + EPS) * g + b
    return yn * _sigmoid(yn)


def _conv_seq_kernel(ua_ref, ub_ref, w_ref, b_ref, lg_ref, lb_ref, y_ref, nb_ref, buf, shifted, *, tt, kw):
    t = pl.program_id(1)
    cc = buf.shape[1]

    @pl.when(t == 0)
    def _():
        buf[0:CONV_HALO, :] = jnp.zeros((CONV_HALO, cc), F32)

    buf[CONV_HALO:CONV_HALO + tt, :] = ua_ref[...] * _sigmoid(ub_ref[...])
    span = shifted.shape[1]
    for r in range(1, 8):
        shifted[r - 1] = buf[r:r + span, :]
    first = CONV_HALO - (kw - 1)

    acc = jnp.broadcast_to(b_ref[...], (tt, cc))
    for j in range(kw):
        q, r = divmod(first + j, 8)
        win = buf[8 * q:8 * q + tt, :] if r == 0 else shifted[r - 1, 8 * q:8 * q + tt, :]
        acc = acc + w_ref[j:j + 1, :] * win
    y_ref[...] = _ln_silu(acc, lg_ref[...], lb_ref[...]).astype(y_ref.dtype)

    @pl.when(t == pl.num_programs(1) - 1)
    def _():
        nb_ref[...] = buf[CONV_HALO + tt - (kw - 1):CONV_HALO + tt, :]

    buf[0:CONV_HALO, :] = buf[tt:tt + CONV_HALO, :]


def _conv_seq(u2, n_seq, seq_len, w_dw, b_dw, ln_g, ln_b):
    kw, cc = w_dw.shape
    assert kw - 1 <= CONV_HALO
    tt = _tile(seq_len, 256)
    assert tt % 8 == 0 and tt >= CONV_HALO
    nt = seq_len // tt
    span = tt + CONV_HALO - 8
    row = lambda v: v.reshape(1, cc)
    vmem = 4 * tt * cc * 4 + 2 * tt * cc * 2 + (CONV_HALO + tt + 7 * span) * cc * 4 + 6 * tt * cc * 4
    return pl.pallas_call(
        functools.partial(_conv_seq_kernel, tt=tt, kw=kw),
        grid=(n_seq, nt),
        in_specs=[pl.BlockSpec((tt, cc), lambda b, t: (b * nt + t, 0)),
                  pl.BlockSpec((tt, cc), lambda b, t: (b * nt + t, 1)),
                  pl.BlockSpec((kw, cc), lambda b, t: (0, 0)),
                  pl.BlockSpec((1, cc), lambda b, t: (0, 0)),
                  pl.BlockSpec((1, cc), lambda b, t: (0, 0)),
                  pl.BlockSpec((1, cc), lambda b, t: (0, 0))],
        out_specs=[pl.BlockSpec((tt, cc), lambda b, t: (b * nt + t, 0)),
                   pl.BlockSpec((None, None, kw - 1, cc), lambda b, t: (0, b, 0, 0))],
        out_shape=[jax.ShapeDtypeStruct((n_seq * seq_len, cc), BF16),
                   jax.ShapeDtypeStruct((1, n_seq, kw - 1, cc), F32)],
        scratch_shapes=[pltpu.VMEM((CONV_HALO + tt, cc), F32), pltpu.VMEM((7, span, cc), F32)],
        compiler_params=_params(("arbitrary", "arbitrary"), vmem),
        name="conv_seq",
    )(u2, u2, w_dw, row(b_dw), row(ln_g), row(ln_b))


def _conv_step_kernel(ua_ref, ub_ref, st_ref, w_ref, b_ref, lg_ref, lb_ref, y_ref, nb_ref, gbuf, ybuf,
                      *, kw, t_len):
    hist = kw - 1
    n_seq, cc = st_ref.shape[1], st_ref.shape[2]
    n_lt = gbuf.shape[0]
    lanes = gbuf.shape[2]
    glu = ua_ref[...] * _sigmoid(ub_ref[...])
    for lt in range(n_lt):
        gbuf[lt] = glu[:, lt * lanes:(lt + 1) * lanes]

    def slab(r):
        if r < hist:
            return st_ref[r]
        step = pl.ds(r - hist, n_seq, stride=t_len)
        return jnp.concatenate([gbuf[lt, step, :] for lt in range(n_lt)], axis=1)

    for t in range(t_len):
        acc = jnp.broadcast_to(b_ref[...], (n_seq, cc))
        for j in range(kw):
            acc = acc + w_ref[j:j + 1, :] * slab(t + j)
        y = _ln_silu(acc, lg_ref[...], lb_ref[...])
        for lt in range(n_lt):
            ybuf[lt, pl.ds(t, n_seq, stride=t_len), :] = y[:, lt * lanes:(lt + 1) * lanes]
    y_ref[...] = jnp.concatenate([ybuf[lt] for lt in range(n_lt)], axis=1)
    for r in range(hist):
        nb_ref[r] = slab(r + t_len)


def _conv_step(u2, state_t, t_len, w_dw, b_dw, ln_g, ln_b):
    kw, cc = w_dw.shape
    n_seq = state_t.shape[2]
    bs = _tile(n_seq, 16)
    tb = bs * t_len
    row = lambda v: v.reshape(1, cc)
    vmem = 4 * tb * cc * 4 + 2 * tb * cc * 4 + 4 * (kw - 1) * bs * cc * 4 + tb * cc * 4 + 4 * tb * cc * 4
    hist_spec = pl.BlockSpec((None, kw - 1, bs, cc), lambda i: (0, 0, i, 0))
    return pl.pallas_call(
        functools.partial(_conv_step_kernel, kw=kw, t_len=t_len),
        grid=(n_seq // bs,),
        in_specs=[pl.BlockSpec((tb, cc), lambda i: (i, 0)),
                  pl.BlockSpec((tb, cc), lambda i: (i, 1)),
                  hist_spec,
                  pl.BlockSpec((kw, cc), lambda i: (0, 0)),
                  pl.BlockSpec((1, cc), lambda i: (0, 0)),
                  pl.BlockSpec((1, cc), lambda i: (0, 0)),
                  pl.BlockSpec((1, cc), lambda i: (0, 0))],
        out_specs=[pl.BlockSpec((tb, cc), lambda i: (i, 0)), hist_spec],
        out_shape=[jax.ShapeDtypeStruct((n_seq * t_len, cc), F32),
                   jax.ShapeDtypeStruct(state_t.shape, F32)],
        scratch_shapes=[pltpu.VMEM((cc // V7X_LANES, tb, V7X_LANES), F32),
                        pltpu.VMEM((cc // V7X_LANES, tb, V7X_LANES), F32)],
        compiler_params=_params(("arbitrary",), vmem),
        name="conv_step",
    )(u2, u2, state_t, w_dw, row(b_dw), row(ln_g), row(ln_b))


def _gla_consts(c):
    n_lev = int(np.log2(c))
    i = np.arange(c)
    mall = [(i[None, :] <= i[:, None])]
    masks = []
    for lev in range(n_lev):
        s = 1 << lev
        m = (i // (2 * s)) * (2 * s) + s - 1
        mall.append(i[None, :] <= m[:, None])
        same = (i[:, None] // (2 * s)) == (i[None, :] // (2 * s))
        upper = ((i[:, None] // s) % 2) == 1
        lower = ((i[None, :] // s) % 2) == 0
        masks.append(same & upper & lower)
    mall = jnp.asarray(np.concatenate(mall, axis=0).astype(np.float32), dtype=BF16)
    masks = jnp.asarray(np.stack(masks).astype(np.float32))
    return mall, masks, n_lev


def _gla_intra(units, mask_ref, c, n_lev):
    nt_dims = (((1,), (1,)), ((), ()))
    tn_dims = (((0,), (0,)), ((), ()))
    staged = []
    for q, k, v, ball in units:
        bc = ball[0:c]
        b_last = bc[c - 1:c, :]
        levels = []
        for lev in range(n_lev):
            bm = ball[(1 + lev) * c:(2 + lev) * c]
            levels.append(((q * jnp.exp(jnp.minimum(bc - bm, 0.0))).astype(BF16),
                           (k * jnp.exp(jnp.minimum(bm - bc, 0.0))).astype(BF16)))
        staged.append(dict(qe=(q * jnp.exp(bc)).astype(BF16), kd=(k * jnp.exp(b_last - bc)).astype(BF16),
                           vb=v.astype(BF16), diag=jnp.sum(q * k, axis=-1, keepdims=True) * v,
                           b_last=b_last, levels=levels))
    atts = []
    for u in staged:
        att = None
        for lev, (ql, kl) in enumerate(u["levels"]):
            a = lax.dot_general(ql, kl, nt_dims, preferred_element_type=F32)
            a = jnp.where(mask_ref[lev] > 0.5, a, 0.0)
            att = a if att is None else att + a
        atts.append(att)
    out = []
    for u, att in zip(staged, atts):
        dk = u["kd"].shape[1]
        dv = u["vb"].shape[1]
        o_intra = _dot(att.astype(BF16), u["vb"]) + u["diag"]
        upd = lax.dot_general(u["kd"], u["vb"], tn_dims, preferred_element_type=F32)
        dec = jnp.exp(jnp.transpose(jnp.broadcast_to(u["b_last"], (dk, dk))))
        out.append((u["qe"], o_intra, upd, jnp.tile(dec, (1, dv // dk))))
    return out


def _gla_finish(o, g, gn):
    return _rms(o, gn) * (g * _sigmoid(g))


GLA_SEQ_CHUNKS = 4
GLA_STEP_SEQS = 4


def _gla_seq_kernel(q_ref, k_ref, v_ref, g_ref, la_ref, gn_ref, mall_ref, mask_ref, y_ref, sf_ref, s_scr,
                    *, c, n_lev, heads):
    t = pl.program_id(1)
    dk = q_ref.shape[1] // heads
    dv = v_ref.shape[1] // heads
    n_ch = q_ref.shape[0] // c

    @pl.when(t == 0)
    def _():
        s_scr[...] = jnp.zeros(s_scr.shape, F32)

    mall = mall_ref[...]
    units = []
    for ci in range(n_ch):
        rows = slice(ci * c, (ci + 1) * c)
        ball = _dot_sel(mall, la_ref[rows, :])
        for h in range(heads):
            ks = slice(h * dk, (h + 1) * dk)
            units.append((q_ref[rows, ks] * (dk ** -0.5), k_ref[rows, ks], v_ref[rows, h * dv:(h + 1) * dv],
                          ball[:, ks]))
    parts = _gla_intra(units, mask_ref, c, n_lev)
    state = [s_scr[h] for h in range(heads)]
    for ci in range(n_ch):
        rows = slice(ci * c, (ci + 1) * c)
        for h in range(heads):
            vs = slice(h * dv, (h + 1) * dv)
            qe, o_intra, upd, dec = parts[ci * heads + h]
            o = o_intra + _dot(qe, state[h].astype(BF16))
            state[h] = state[h] * dec + upd
            y_ref[rows, vs] = _gla_finish(o, g_ref[rows, vs], gn_ref[...]).astype(y_ref.dtype)
    for h in range(heads):
        s_scr[h] = state[h]

    @pl.when(t == pl.num_programs(1) - 1)
    def _():
        sf_ref[...] = s_scr[...]


def _gla_seq(u2, la, n_seq, seq_len, gn_g, cc, qk):
    heads = GLA_HEADS
    dk = qk // heads
    dv = cc // heads
    c = int(np.gcd(seq_len, GLA_CHUNK))
    n_ch = GLA_SEQ_CHUNKS if (seq_len // c) % GLA_SEQ_CHUNKS == 0 else 1
    tb = n_ch * c
    nt = seq_len // tb
    mall, masks, n_lev = _gla_consts(c)
    qb = 2 * cc // qk
    vb = (2 * cc + 2 * qk) // cc
    vmem = 2 * tb * (3 * qk + 3 * cc) * 4 + 3 * heads * dk * dv * 4 + 96 * n_ch * heads * c * max(dv, 128) * 4
    return pl.pallas_call(
        functools.partial(_gla_seq_kernel, c=c, n_lev=n_lev, heads=heads),
        grid=(n_seq, nt),
        in_specs=[pl.BlockSpec((tb, qk), lambda b, t: (b * nt + t, qb)),
                  pl.BlockSpec((tb, qk), lambda b, t: (b * nt + t, qb + 1)),
                  pl.BlockSpec((tb, cc), lambda b, t: (b * nt + t, vb)),
                  pl.BlockSpec((tb, cc), lambda b, t: (b * nt + t, vb + 1)),
                  pl.BlockSpec((tb, qk), lambda b, t: (b * nt + t, 0)),
                  pl.BlockSpec((1, dv), lambda b, t: (0, 0)),
                  pl.BlockSpec(mall.shape, lambda b, t: (0, 0)),
                  pl.BlockSpec(masks.shape, lambda b, t: (0, 0, 0))],
        out_specs=[pl.BlockSpec((tb, cc), lambda b, t: (b * nt + t, 0)),
                   pl.BlockSpec((None, None, heads, dk, dv), lambda b, t: (0, b, 0, 0, 0))],
        out_shape=[jax.ShapeDtypeStruct((n_seq * seq_len, cc), BF16),
                   jax.ShapeDtypeStruct((1, n_seq, heads, dk, dv), F32)],
        scratch_shapes=[pltpu.VMEM((heads, dk, dv), F32)],
        compiler_params=_params(("arbitrary", "arbitrary"), vmem),
        name="gla_seq",
    )(u2, u2, u2, u2, la, gn_g.reshape(1, dv), mall, masks)


def _gla_step_kernel(q_ref, k_ref, v_ref, g_ref, la_ref, st_ref, gn_ref, mall_ref, mask_ref, y_ref, sn_ref,
                     *, c, n_lev, heads):
    dk = q_ref.shape[2] // heads
    dv = v_ref.shape[2] // heads
    mall = mall_ref[...]

    n_un = GLA_STEP_SEQS if q_ref.shape[0] % GLA_STEP_SEQS == 0 else 1

    def body(t, carry):
        units = []
        for ss in range(n_un):
            s = t * n_un + ss
            ball = _dot_sel(mall, la_ref[s])
            for h in range(heads):
                ks = slice(h * dk, (h + 1) * dk)
                units.append((q_ref[s, :, ks] * (dk ** -0.5), k_ref[s, :, ks], v_ref[s, :, h * dv:(h + 1) * dv],
                              ball[:, ks]))
        parts = _gla_intra(units, mask_ref, c, n_lev)
        for ss in range(n_un):
            s = t * n_un + ss
            for h in range(heads):
                vs = slice(h * dv, (h + 1) * dv)
                qe, o_intra, upd, dec = parts[ss * heads + h]
                s0 = st_ref[s, h]
                o = o_intra + _dot(qe, s0.astype(BF16))
                sn_ref[s, h] = s0 * dec + upd
                y_ref[s, :, vs] = _gla_finish(o, g_ref[s, :, vs], gn_ref[...])
        return carry

    lax.fori_loop(0, q_ref.shape[0] // n_un, body, 0)


def _gla_step(u3, la3, state, gn_g, cc, qk):
    n_seq, t_len, _ = u3.shape
    heads = GLA_HEADS
    dk = qk // heads
    dv = cc // heads
    assert t_len <= GLA_CHUNK and GLA_CHUNK % t_len == 0, "single-chunk path"
    mall, masks, n_lev = _gla_consts(t_len)
    bs = min(8, n_seq)
    qb = 2 * cc // qk
    vb = (2 * cc + 2 * qk) // cc
    vmem = 2 * bs * t_len * (3 * qk + 3 * cc) * 4 + 4 * bs * heads * dk * dv * 4 + 64 * 8 * max(dv, 128) * 4
    return pl.pallas_call(
        functools.partial(_gla_step_kernel, c=t_len, n_lev=n_lev, heads=heads),
        grid=(n_seq // bs,),
        in_specs=[pl.BlockSpec((bs, t_len, qk), lambda i: (i, 0, qb)),
                  pl.BlockSpec((bs, t_len, qk), lambda i: (i, 0, qb + 1)),
                  pl.BlockSpec((bs, t_len, cc), lambda i: (i, 0, vb)),
                  pl.BlockSpec((bs, t_len, cc), lambda i: (i, 0, vb + 1)),
                  pl.BlockSpec((bs, t_len, qk), lambda i: (i, 0, 0)),
                  pl.BlockSpec((None, bs, heads, dk, dv), lambda i: (0, i, 0, 0, 0)),
                  pl.BlockSpec((1, dv), lambda i: (0, 0)),
                  pl.BlockSpec(mall.shape, lambda i: (0, 0)),
                  pl.BlockSpec(masks.shape, lambda i: (0, 0, 0))],
        out_specs=[pl.BlockSpec((bs, t_len, cc), lambda i: (i, 0, 0)),
                   pl.BlockSpec((None, bs, heads, dk, dv), lambda i: (0, i, 0, 0, 0))],
        out_shape=[jax.ShapeDtypeStruct((n_seq, t_len, cc), F32),
                   jax.ShapeDtypeStruct((1, n_seq, heads, dk, dv), F32)],
        compiler_params=_params(("arbitrary",), vmem),
        name="gla_step",
    )(u3, u3, u3, u3, la3, state, gn_g.reshape(1, dv), mall, masks)


def _out_kernel(x_ref, yc_ref, yg_ref, wo_ref, g1_ref, sh_ref, sc_ref, n2_ref, wr_ref, br_ref,
                x1_ref, h2_ref, ti_ref, tw_ref, *, tm, seq_len):
    i = pl.program_id(0)
    cc = yc_ref.shape[1]
    n_exp = wr_ref.shape[1]
    mix = _dot(yc_ref[...].astype(BF16), wo_ref[0:cc, :]) + _dot(yg_ref[...].astype(BF16), wo_ref[cc:, :])
    x1 = x_ref[...] + _expand_rows(g1_ref, i, tm, seq_len) * mix
    x1_ref[...] = x1
    h2 = _rms(x1, n2_ref[...]) * (1.0 + _expand_rows(sc_ref, i, tm, seq_len)) + _expand_rows(sh_ref, i, tm, seq_len)
    h2_ref[...] = h2

    logits = _dot(h2.astype(BF16), wr_ref[...].astype(BF16)) + br_ref[...]
    lane = lax.broadcasted_iota(I32, logits.shape, 1)
    vals = []
    for r in range(TOP_K):
        m = jnp.max(logits, axis=-1, keepdims=True)
        idx = jnp.min(jnp.where(logits == m, lane, n_exp), axis=-1, keepdims=True)
        vals.append(m)
        ti_ref[:, r:r + 1] = idx
        logits = jnp.where(lane == idx, -jnp.inf, logits)
    ex = [jnp.exp(v - vals[0]) for v in vals]
    den = ex[0] + ex[1] + ex[2] + ex[3]
    for r in range(TOP_K):
        tw_ref[:, r:r + 1] = ex[r] / den


def _out_proj(x2, yc, yg, wo_bf, g1, sh2, sc2, n2, w_router, b_router, seq_len):
    n, d = x2.shape
    cc = yc.shape[1]
    n_seq = g1.shape[0]
    n_exp = w_router.shape[1]
    tm = _tile(n, 256)
    vmem = (6 * tm * d * 4 + 4 * tm * cc * 4 + 2 * 2 * cc * d * 2 + 8 * tm * d * 4)
    tok = lambda i: (i, 0)
    fix = lambda i: (0, 0)
    return pl.pallas_call(
        functools.partial(_out_kernel, tm=tm, seq_len=seq_len),
        grid=(n // tm,),
        in_specs=[pl.BlockSpec((tm, d), tok),
                  pl.BlockSpec((tm, cc), tok),
                  pl.BlockSpec((tm, cc), tok),
                  pl.BlockSpec((2 * cc, d), fix),
                  _table_spec(n_seq, d, tm, seq_len),
                  _table_spec(n_seq, d, tm, seq_len),
                  _table_spec(n_seq, d, tm, seq_len),
                  pl.BlockSpec((1, d), fix),
                  pl.BlockSpec((d, n_exp), fix),
                  pl.BlockSpec((1, n_exp), fix)],
        out_specs=[pl.BlockSpec((tm, d), tok),
                   pl.BlockSpec((tm, d), tok),
                   pl.BlockSpec((tm, TOP_K), tok),
                   pl.BlockSpec((tm, TOP_K), tok)],
        out_shape=[jax.ShapeDtypeStruct((n, d), F32),
                   jax.ShapeDtypeStruct((n, d), F32),
                   jax.ShapeDtypeStruct((n, TOP_K), I32),
                   jax.ShapeDtypeStruct((n, TOP_K), F32)],
        compiler_params=_params(("arbitrary",), vmem),
        name="out_proj",
    )(x2, yc, yg, wo_bf, g1, sh2, sc2, n2.reshape(1, d), w_router, b_router.reshape(1, n_exp))


def _rank_kernel(ti_ref, slot_ref, cnt_ref, off_ref, carry, offs, *, tr, n_exp, group):
    p = pl.program_id(0)
    i = pl.program_id(1)
    lane = lax.broadcasted_iota(I32, (tr, n_exp), 1)
    ti = ti_ref[...]
    hot = jnp.zeros((tr, n_exp), F32)
    for k in range(TOP_K):
        hot = hot + (lane == ti[:, k:k + 1]).astype(F32)

    @pl.when((p == 0) & (i == 0))
    def _():
        carry[...] = jnp.zeros(carry.shape, F32)

    @pl.when((p == 1) & (i == 0))
    def _():
        cnt = carry[0:1, :]
        padded = jnp.ceil(cnt * (1.0 / group)) * group
        r = lax.broadcasted_iota(I32, (n_exp, n_exp), 0)
        c = lax.broadcasted_iota(I32, (n_exp, n_exp), 1)
        before = (r < c).astype(BF16)
        hi, mid, lo = _split3(jnp.broadcast_to(padded, (8, n_exp)))
        off = _dot(hi, before) + _dot(mid, before) + _dot(lo, before)
        offs[...] = off
        cnt_ref[...] = cnt.astype(I32)
        off_ref[...] = off[0:1, :].astype(I32)
        carry[...] = jnp.zeros(carry.shape, F32)

    @pl.when(p == 1)
    def _():
        r = lax.broadcasted_iota(I32, (tr, tr), 0)
        c = lax.broadcasted_iota(I32, (tr, tr), 1)
        earlier = (c < r).astype(BF16)
        tot = _dot(earlier, hot.astype(BF16)) + carry[0:1, :] + offs[0:1, :]
        for k in range(TOP_K):
            sel = jnp.where(lane == ti[:, k:k + 1], tot, 0.0)
            slot_ref[:, k:k + 1] = jnp.sum(sel, axis=-1, keepdims=True).astype(I32)

    carry[...] = carry[...] + jnp.sum(hot, axis=0, keepdims=True)


def _rank(top_i, n_exp, group):
    n = top_i.shape[0]
    tr = _tile(n, 512)
    return pl.pallas_call(
        functools.partial(_rank_kernel, tr=tr, n_exp=n_exp, group=group),
        grid=(2, n // tr),
        in_specs=[pl.BlockSpec((tr, TOP_K), lambda p, i: (i, 0))],
        out_specs=[pl.BlockSpec((tr, TOP_K), lambda p, i: (i * p, 0)),
                   pl.BlockSpec((1, n_exp), lambda p, i: (0, 0)),
                   pl.BlockSpec((1, n_exp), lambda p, i: (0, 0))],
        out_shape=[jax.ShapeDtypeStruct((n, TOP_K), I32),
                   jax.ShapeDtypeStruct((1, n_exp), I32),
                   jax.ShapeDtypeStruct((1, n_exp), I32)],
        scratch_shapes=[pltpu.VMEM((8, n_exp), F32), pltpu.VMEM((8, n_exp), F32)],
        compiler_params=_params(("arbitrary", "arbitrary"), 8 * tr * tr * 4),
        name="rank",
    )(top_i)


def _scatter_rows(slot_ref, h_ref, xs_ref, sem, td):
    def issue(r, carry):
        for k in range(TOP_K):
            s = slot_ref[0, r * TOP_K + k]
            pltpu.make_async_copy(h_ref.at[pl.ds(r, 1), :], xs_ref.at[pl.ds(s, 1), :], sem).start()
        return carry

    lax.fori_loop(0, td, issue, 0)
    for k in range(TOP_K):
        pltpu.make_async_copy(h_ref, xs_ref.at[pl.ds(0, td), :], sem).wait()


def _dispatch_kernel(cnt, off, slot_ref, ha_ref, hb_ref, xs_ref, zbuf, sem, zsem, *, td, group, n_a):
    n_exp = cnt.shape[0]
    n_rows = xs_ref.shape[0]
    i = pl.program_id(0)

    def padded(e):
        return (cnt[e] + group - 1) // group * group

    def z_row(row):
        return pltpu.make_async_copy(zbuf.at[pl.ds(0, 1), :], xs_ref.at[pl.ds(row, 1), :], zsem)

    def z_block(row):
        return pltpu.make_async_copy(zbuf, xs_ref.at[pl.ds(pl.multiple_of(row, group), group), :], zsem)

    def for_pad_rows(fn):
        def expert(e, carry):
            lax.fori_loop(off[e] + cnt[e], off[e] + padded(e), lambda r, c: (fn(r), c)[1], 0)
            return carry
        lax.fori_loop(0, n_exp, expert, 0)

    def for_tail_blocks(fn):
        first = (off[n_exp - 1] + padded(n_exp - 1)) // group
        lax.fori_loop(first, n_rows // group, lambda b, c: (fn(b * group), c)[1], 0)

    @pl.when(i == 0)
    def _():
        zbuf[...] = jnp.zeros(zbuf.shape, zbuf.dtype)
        for_pad_rows(lambda r: z_row(r).start())
        for_tail_blocks(lambda r: z_block(r).start())

    @pl.when(i < n_a)
    def _():
        _scatter_rows(slot_ref, ha_ref, xs_ref, sem, td)

    @pl.when(i >= n_a)
    def _():
        _scatter_rows(slot_ref, hb_ref, xs_ref, sem, td)

    @pl.when(i == pl.num_programs(0) - 1)
    def _():
        for_pad_rows(lambda r: z_row(0).wait())
        for_tail_blocks(lambda r: z_block(0).wait())


def _dispatch(slots, h_a, h_b, cnt, off, n_rows, group):
    d = h_a.shape[1]
    td = _tile(int(np.gcd(h_a.shape[0], h_b.shape[0])), 256)
    n_a, n_b = h_a.shape[0] // td, h_b.shape[0] // td
    grid_spec = pltpu.PrefetchScalarGridSpec(
        num_scalar_prefetch=2,
        grid=(n_a + n_b,),
        in_specs=[pl.BlockSpec((None, 1, td * TOP_K), lambda i, *_: (i, 0, 0), memory_space=pltpu.SMEM),
                  pl.BlockSpec((td, d), lambda i, *_: (jnp.minimum(i, n_a - 1), 0)),
                  pl.BlockSpec((td, d), lambda i, *_: (jnp.maximum(i - n_a, 0), 0))],
        out_specs=pl.BlockSpec(memory_space=pl.ANY),
        scratch_shapes=[pltpu.VMEM((group, d), h_a.dtype), pltpu.SemaphoreType.DMA, pltpu.SemaphoreType.DMA])
    return pl.pallas_call(
        functools.partial(_dispatch_kernel, td=td, group=group, n_a=n_a),
        grid_spec=grid_spec,
        out_shape=jax.ShapeDtypeStruct((n_rows, d), h_a.dtype),
        compiler_params=_params(("arbitrary",), 4 * td * d * 4 + group * d * 4),
        name="dispatch",
    )(cnt, off, slots.reshape(n_a + n_b, 1, td * TOP_K), h_a, h_b)


MOE_CHUNK = 2048
MOE_SUB = 128
MOE_MM = 2
MOE_TF = 256
MOE_TN = 256


def _for_row_blocks(n_units, body):
    n = n_units // MOE_MM

    def block(b):
        body(b * MOE_MM, MOE_MM)

    def trip(t, carry):
        for u in range(4):
            block(t * 4 + u)
        return carry

    lax.fori_loop(0, n // 4, trip, 0)

    @pl.when((n // 2) % 2 == 1)
    def _():
        block((n // 4) * 4)
        block((n // 4) * 4 + 1)

    @pl.when(n % 2 == 1)
    def _():
        block(n - 1)

    @pl.when(n_units % MOE_MM == 1)
    def _():
        body(n_units - 1, 1)


def _moe_kernel(che, chrow, chn, tail, xs_ref, wg_ref, wu_ref, bg_ref, bu_ref, wd_ref, bd_ref, yb_ref,
                xbuf, xb, act, wgb, wub, wdb, obuf, semx, semo, *, n_f, n_n, n_chunks):
    c = pl.program_id(0)
    j = pl.program_id(1)
    nsb = chn[c]
    sub = MOE_SUB

    def x_copy(cc, i):
        row = pl.multiple_of(chrow[cc] + i * sub, sub)
        return pltpu.make_async_copy(xs_ref.at[pl.ds(row, sub), :],
                                     xbuf.at[pl.ds(pl.multiple_of(i * sub, sub), sub), :], semx)

    def x_fetch(cc):
        lax.fori_loop(0, chn[cc], lambda i, carry: (x_copy(cc, i).start(), carry)[1], 0)

    def o_copy(slot, i, jj):
        row = pl.multiple_of(chrow[c] + i * sub, sub)
        col = pl.multiple_of(jj * MOE_TN, MOE_TN)
        return pltpu.make_async_copy(obuf.at[slot, pl.ds(pl.multiple_of(i * sub, sub), sub), :],
                                     yb_ref.at[pl.ds(row, sub), pl.ds(col, MOE_TN)], semo.at[slot])

    @pl.when((c == 0) & (j == 0))
    def _():
        x_fetch(0)

    @pl.when(j == 0)
    def _():
        lax.fori_loop(0, nsb, lambda i, carry: (x_copy(c, 0).wait(), carry)[1], 0)

        def cast(i, carry):
            rows = pl.ds(pl.multiple_of(i * sub, sub), sub)
            xb[rows, :] = xbuf[rows, :].astype(BF16)
            return carry

        lax.fori_loop(0, nsb, cast, 0)

    @pl.when((j == 1) & (c + 1 < n_chunks))
    def _():
        x_fetch(jnp.minimum(c + 1, n_chunks - 1))

    @pl.when((j < n_f) & (nsb > 0))
    def _():
        wgb[...] = wg_ref[...].astype(BF16)
        wub[...] = wu_ref[...].astype(BF16)

        def row_block(u, n):
            rows = pl.ds(pl.multiple_of(u * sub, sub), n * sub)
            x = xb[rows, :]
            gate = jnp.minimum(_dot(x, wgb[...]) + bg_ref[...], SWIGLU_LIMIT)
            up = jnp.clip(_dot(x, wub[...]) + bu_ref[...], -SWIGLU_LIMIT, SWIGLU_LIMIT)
            glu = gate * _sigmoid(SWIGLU_ALPHA * gate)
            act[j, rows, :] = ((up + 1.0) * glu).astype(BF16)

        _for_row_blocks(nsb, row_block)

    @pl.when((j >= n_f) & (nsb > 0))
    def _():
        jj = j - n_f
        slot = jj % 2
        wdb[...] = wd_ref[...].astype(BF16)

        def retire(s):
            def one(i, carry):
                o_copy(s, 0, 0).wait()
                return carry
            lax.fori_loop(0, nsb, one, 0)

        @pl.when(jj >= 2)
        def _():
            retire(slot)

        def row_block(u, n):
            rows = pl.ds(pl.multiple_of(u * sub, sub), n * sub)
            a = jnp.concatenate([act[f, rows, :] for f in range(n_f)], axis=1)
            obuf[slot, rows, :] = _dot(a, wdb[...]) + bd_ref[...]
            for k in range(n):
                o_copy(slot, u + k, jj).start()

        _for_row_blocks(nsb, row_block)

        @pl.when(jj == n_n - 1)
        def _():
            retire(slot)
            if n_n >= 2:
                retire(1 - slot)

    @pl.when((c == n_chunks - 1) & (j == n_f + n_n - 1))
    def _():
        n_tail = (yb_ref.shape[0] - tail[0]) // sub
        obuf[0, 0:sub, :] = jnp.zeros((sub, MOE_TN), F32)

        def z_copy(i, js):
            row = pl.multiple_of(tail[0] + i * sub, sub)
            return pltpu.make_async_copy(obuf.at[0, pl.ds(0, sub), :],
                                         yb_ref.at[pl.ds(row, sub), pl.ds(js * MOE_TN, MOE_TN)], semo.at[0])

        def fill(i, carry):
            for js in range(n_n):
                z_copy(i, js).start()
            return carry

        def drain(i, carry):
            for js in range(n_n):
                z_copy(0, js).wait()
            return carry

        lax.fori_loop(0, n_tail, fill, 0)
        lax.fori_loop(0, n_tail, drain, 0)


def _moe(xs, yb_rows, ch_e, ch_row, ch_n, tail, w_gate, b_gate, w_up, b_up, w_down, b_down):
    n_exp, d, f = w_gate.shape
    n_f = f // MOE_TF
    n_n = d // MOE_TN
    n_chunks = ch_e.shape[0]

    def w1(c, j, che, chrow, chn, tail):
        return (che[c], 0, jnp.where(chn[c] > 0, jnp.minimum(j, n_f - 1), n_f - 1))

    def w2(c, j, che, chrow, chn, tail):
        return (che[c], 0, jnp.where(chn[c] > 0, jnp.maximum(j - n_f, 0), n_n - 1))

    vmem = (MOE_CHUNK * d * 4 + MOE_CHUNK * d * 2 + MOE_CHUNK * f * 2
            + 2 * 2 * d * MOE_TF * 4 + 2 * f * MOE_TN * 4 + 2 * d * MOE_TF * 2 + f * MOE_TN * 2
            + 2 * MOE_CHUNK * MOE_TN * 4 + 8 * MOE_MM * MOE_SUB * max(d, f) * 4)
    grid_spec = pltpu.PrefetchScalarGridSpec(
        num_scalar_prefetch=4,
        grid=(n_chunks, n_f + n_n),
        in_specs=[pl.BlockSpec(memory_space=pl.ANY),
                  pl.BlockSpec((None, d, MOE_TF), w1),
                  pl.BlockSpec((None, d, MOE_TF), w1),
                  pl.BlockSpec((None, 1, MOE_TF), w1),
                  pl.BlockSpec((None, 1, MOE_TF), w1),
                  pl.BlockSpec((None, f, MOE_TN), w2),
                  pl.BlockSpec((None, 1, MOE_TN), w2)],
        out_specs=pl.BlockSpec(memory_space=pl.ANY),
        scratch_shapes=[pltpu.VMEM((MOE_CHUNK, d), F32),
                        pltpu.VMEM((MOE_CHUNK, d), BF16),
                        pltpu.VMEM((n_f, MOE_CHUNK, MOE_TF), BF16),
                        pltpu.VMEM((d, MOE_TF), BF16),
                        pltpu.VMEM((d, MOE_TF), BF16),
                        pltpu.VMEM((f, MOE_TN), BF16),
                        pltpu.VMEM((2, MOE_CHUNK, MOE_TN), F32),
                        pltpu.SemaphoreType.DMA,
                        pltpu.SemaphoreType.DMA((2,))])
    return pl.pallas_call(
        functools.partial(_moe_kernel, n_f=n_f, n_n=n_n, n_chunks=n_chunks),
        grid_spec=grid_spec,
        out_shape=jax.ShapeDtypeStruct((yb_rows, d), F32),
        compiler_params=_params(("arbitrary", "arbitrary"), vmem),
        name="moe",
    )(ch_e, ch_row, ch_n, tail, xs, w_gate, w_up, b_gate.reshape(n_exp, 1, f), b_up.reshape(n_exp, 1, f),
      w_down, b_down.reshape(n_exp, 1, d))


def _chunk_table(cnt, off, n_chunks):
    n_exp = cnt.shape[0]
    per = MOE_CHUNK // MOE_SUB
    nsb = (cnt + MOE_SUB - 1) // MOE_SUB
    nch = (nsb + per - 1) // per
    end = jnp.cumsum(nch)
    start = end - nch
    cidx = jnp.arange(n_chunks, dtype=I32)
    owner = lambda ci: jnp.minimum(jnp.sum((end[None, :] <= ci[:, None]).astype(I32), axis=1), n_exp - 1)
    e = owner(cidx)
    used = cidx < end[-1]
    last_e = owner(end[-1:] - 1)[0]
    local = cidx - start[e]
    ch_e = jnp.where(used, e, last_e)
    ch_row = jnp.where(used, off[e] + local * MOE_CHUNK, 0).astype(I32)
    ch_n = jnp.where(used, jnp.minimum(per, nsb[e] - local * per), 0).astype(I32)
    tail = (off[-1] + nsb[-1] * MOE_SUB).astype(I32).reshape(1)
    return ch_e, ch_row, ch_n, tail


def _combine_kernel(slot_ref, next_slot_ref, tw_ref, x1_ref, g2_ref, gf_ref, yb_ref, y_ref, gbuf, sem,
                    *, tc, seq_len):
    i = pl.program_id(0)
    cur = i % 2

    def gather(s_ref, half):
        def rows(r, carry):
            for k in range(TOP_K):
                s = s_ref[0, r * TOP_K + k]
                pltpu.make_async_copy(yb_ref.at[pl.ds(s, 1), :], gbuf.at[half, k, pl.ds(r, 1), :],
                                      sem.at[half]).start()
            return carry
        lax.fori_loop(0, tc, rows, 0)

    @pl.when(i == 0)
    def _():
        gather(slot_ref, 0)

    @pl.when(i + 1 < pl.num_programs(0))
    def _():
        gather(next_slot_ref, 1 - cur)

    for k in range(TOP_K):
        pltpu.make_async_copy(yb_ref.at[pl.ds(0, tc), :], gbuf.at[cur, k], sem.at[cur]).wait()
    tw = tw_ref[...]
    y = tw[:, 0:1] * gbuf[cur, 0]
    for k in range(1, TOP_K):
        y = y + tw[:, k:k + 1] * gbuf[cur, k]
    x2 = x1_ref[...] + _expand_rows(g2_ref, i, tc, seq_len) * y
    y_ref[...] = _rms(x2, gf_ref[...])


def _combine(slots, top_w, x1, g2, gf, yb, seq_len):
    n, d = x1.shape
    n_seq = g2.shape[0]
    tc = _tile(n, 128)
    nt = n // tc
    vmem = 2 * TOP_K * tc * d * 4 + 4 * tc * d * 4 + 6 * tc * d * 4
    slots3 = slots.reshape(nt, 1, tc * TOP_K)
    return pl.pallas_call(
        functools.partial(_combine_kernel, tc=tc, seq_len=seq_len),
        grid=(nt,),
        in_specs=[pl.BlockSpec((None, 1, tc * TOP_K), lambda i: (i, 0, 0), memory_space=pltpu.SMEM),
                  pl.BlockSpec((None, 1, tc * TOP_K), lambda i: (jnp.minimum(i + 1, nt - 1), 0, 0),
                               memory_space=pltpu.SMEM),
                  pl.BlockSpec((tc, TOP_K), lambda i: (i, 0)),
                  pl.BlockSpec((tc, d), lambda i: (i, 0)),
                  _table_spec(n_seq, d, tc, seq_len),
                  pl.BlockSpec((1, d), lambda i: (0, 0)),
                  pl.BlockSpec(memory_space=pl.ANY)],
        out_specs=pl.BlockSpec((tc, d), lambda i: (i, 0)),
        out_shape=jax.ShapeDtypeStruct((n, d), F32),
        scratch_shapes=[pltpu.VMEM((2, TOP_K, tc, d), F32), pltpu.SemaphoreType.DMA((2,))],
        compiler_params=_params(("arbitrary",), vmem),
        name="combine",
    )(slots3, slots3, top_w, x1, g2, gf.reshape(1, d), yb)


def _layer(xp, xs, cp, cs, conv_state, gla_state, w):
    (w_ada, b_ada, norm1_g, w_in, w_dw, b_dw, conv_ln_g, conv_ln_b, w_alpha, b_alpha, gla_norm_g, w_out,
     norm2_g, w_router, b_router, w_gate, b_gate, w_up, b_up, w_down, b_down, final_g) = w
    bp, tp, d = xp.shape
    bs, ts, _ = xs.shape
    cc = w_dw.shape[1]
    qk = w_alpha.shape[1]
    n_exp = w_router.shape[1]
    n_main = 2 * cc + 2 * qk + 2 * cc
    np_, ns_ = bp * tp, bs * ts

    mod = _ada(jnp.concatenate([cp, cs], axis=0), w_ada, b_ada)
    mods_p = [mod[:bp, i * d:(i + 1) * d] for i in range(6)]
    mods_s = [mod[bp:, i * d:(i + 1) * d] for i in range(6)]
    w_in_t = w_in.T
    w_low = w_in_t[n_main:]
    w_main_bf = w_in_t[:n_main].astype(BF16)
    wo_bf = w_out.astype(BF16)

    xp2 = xp.reshape(np_, d)
    xs2 = xs.reshape(ns_, d)

    u_p, la_p = _in_proj(xp2, mods_p[0], mods_p[1], norm1_g, w_main_bf, w_low, w_alpha, b_alpha, tp)
    yc_p, conv_p = _conv_seq(u_p, bp, tp, w_dw, b_dw, conv_ln_g, conv_ln_b)
    yg_p, gla_p = _gla_seq(u_p, la_p, bp, tp, gla_norm_g, cc, qk)
    x1_p, h2_p, ti_p, tw_p = _out_proj(xp2, yc_p, yg_p, wo_bf, mods_p[2], mods_p[3], mods_p[4], norm2_g,
                                       w_router, b_router, tp)

    u_s, la_s = _in_proj(xs2, mods_s[0], mods_s[1], norm1_g, w_main_bf, w_low, w_alpha, b_alpha, ts)
    yc_s, conv_s = _conv_step(u_s, jnp.transpose(conv_state, (0, 2, 1, 3)), ts, w_dw, b_dw, conv_ln_g, conv_ln_b)
    conv_s = jnp.transpose(conv_s, (0, 2, 1, 3))
    yg_s, gla_s = _gla_step(u_s.reshape(bs, ts, n_main), la_s.reshape(bs, ts, qk), gla_state, gla_norm_g, cc, qk)
    x1_s, h2_s, ti_s, tw_s = _out_proj(xs2, yc_s, yg_s.reshape(ns_, cc), wo_bf, mods_s[2],
                                       mods_s[3], mods_s[4], norm2_g, w_router, b_router, ts)

    n_asg = (np_ + ns_) * TOP_K
    slots, cnt, off = _rank(jnp.concatenate([ti_p, ti_s], axis=0), n_exp, MOE_SUB)
    rows = -(-(n_asg + n_exp * MOE_SUB) // MOE_SUB) * MOE_SUB
    xsort = _dispatch(slots, h2_p, h2_s, cnt[0], off[0], rows, MOE_SUB)
    n_chunks = (n_asg + n_exp * MOE_SUB) // MOE_CHUNK + n_exp
    ch_e, ch_row, ch_n, tail = _chunk_table(cnt[0], off[0], n_chunks)
    yb = _moe(xsort, rows, ch_e, ch_row, ch_n, tail, w_gate, b_gate, w_up, b_up, w_down, b_down)

    y_p = _combine(slots[:np_], tw_p, x1_p, mods_p[5], final_g, yb, tp)
    y_s = _combine(slots[np_:], tw_s, x1_s, mods_s[5], final_g, yb, ts)
    return y_p.reshape(bp, tp, d), y_s.reshape(bs, ts, d), conv_p, gla_p, conv_s, gla_s


def kernel(x_prompt, x_sample, c_prompt, c_sample, state_conv, state_gla, w_ada, b_ada, norm1_g, w_in, w_dw, b_dw,
           conv_ln_g, conv_ln_b, w_alpha, b_alpha, gla_norm_g, w_out, norm2_g, w_router, b_router, w_gate, b_gate,
           w_up, b_up, w_down, b_down, final_norm_g):
    assert w_ada.shape[0] == 1, "the final norm is fused into the (single) layer"
    first = lambda a: a.reshape(a.shape[1:])
    w = tuple(first(a) for a in (w_ada, b_ada, norm1_g, w_in, w_dw, b_dw, conv_ln_g, conv_ln_b, w_alpha, b_alpha,
                                 gla_norm_g, w_out, norm2_g, w_router, b_router, w_gate, b_gate, w_up, b_up,
                                 w_down, b_down)) + (final_norm_g,)
    return _layer(x_prompt, x_sample, c_prompt, c_sample, state_conv, state_gla, w)
```

```python
import functools

import numpy as np
import jax
import jax.numpy as jnp
from jax import lax
from jax.experimental import pallas as pl
from jax.experimental.pallas import tpu as pltpu

F32 = jnp.float32
BF16 = jnp.bfloat16
I32 = jnp.int32

EPS = 1e-5
GLA_HEADS = 4
GLA_GATE_NORM = 16.0
GLA_CHUNK = 64
TOP_K = 4
SWIGLU_LIMIT = 7.0
SWIGLU_ALPHA = 1.702

V7X_LANES = 128
V7X_VMEM_BYTES = 64 * 1024 * 1024
VMEM_CAP_BYTES = V7X_VMEM_BYTES - 6 * 1024 * 1024


def _params(semantics, vmem_bytes):
    limit = int(min(VMEM_CAP_BYTES, max(vmem_bytes * 5 // 4 + (4 << 20), 16 << 20)))
    return pltpu.CompilerParams(dimension_semantics=semantics, vmem_limit_bytes=limit)


def _tile(n, want):
    t = min(want, n)
    while n % t or t % 8:
        t -= 8 if t % 8 == 0 else t % 8
        assert t > 0, (n, want)
    return t


NT_DIMS = (((1,), (1,)), ((), ()))


def _dot(a, b):
    return jnp.dot(a, b, preferred_element_type=F32)


def _split3(v):
    hi = v.astype(BF16)
    r = v - hi.astype(F32)
    mid = r.astype(BF16)
    lo = (r - mid.astype(F32)).astype(BF16)
    return hi, mid, lo


def _dot_sel(m01, v):
    hi, mid, lo = _split3(v)
    return _dot(m01, hi) + _dot(m01, mid) + _dot(m01, lo)


def _sigmoid(x):
    return jax.nn.sigmoid(x)


def _rms(x, g):
    return x * lax.rsqrt(jnp.mean(x * x, axis=-1, keepdims=True) + EPS) * g


def _expand_rows(m_ref, i, tm, seq_len):
    if seq_len >= tm:
        return m_ref[pl.ds((i * tm) // seq_len, 1), :]
    nseq = tm // seq_len
    r = lax.broadcasted_iota(I32, (tm, nseq), 0)
    c = lax.broadcasted_iota(I32, (tm, nseq), 1) * seq_len
    sel = ((r >= c) & (r < c + seq_len)).astype(BF16)
    return _dot_sel(sel, m_ref[...])


def _table_spec(n_seq, d, tm, seq_len):
    if seq_len >= tm:
        return pl.BlockSpec((n_seq, d), lambda i, *_: (0, 0))
    return pl.BlockSpec((tm // seq_len, d), lambda i, *_: (i, 0))


def _ada_kernel(c_ref, w_ref, b_ref, o_ref):
    c = c_ref[...]
    s = (c * _sigmoid(c)).astype(BF16)
    o_ref[...] = _dot(s, w_ref[...].astype(BF16)) + b_ref[...]


def _ada(c_all, w_ada, b_ada):
    bc, d = c_all.shape
    n6 = w_ada.shape[1]
    tn = min(1024, n6)
    return pl.pallas_call(
        _ada_kernel,
        grid=(n6 // tn,),
        in_specs=[pl.BlockSpec((bc, d), lambda j: (0, 0)),
                  pl.BlockSpec((d, tn), lambda j: (0, j)),
                  pl.BlockSpec((1, tn), lambda j: (0, j))],
        out_specs=pl.BlockSpec((bc, tn), lambda j: (0, j)),
        out_shape=jax.ShapeDtypeStruct((bc, n6), F32),
        compiler_params=_params(("arbitrary",), 2 * d * tn * 4 + d * tn * 2 + 4 * bc * (d + tn) * 4),
        name="ada",
    )(c_all, w_ada, b_ada.reshape(1, n6))


IN_TN = 512


def _in_kernel(x_ref, sh_ref, sc_ref, g_ref, w_hbm, wlow_ref, wal_ref, bal_ref, u_ref, la_ref, w_vmem, sem,
               *, tm, seq_len):
    i = pl.program_id(0)
    load_w = pltpu.make_async_copy(w_hbm, w_vmem, sem)

    @pl.when(i == 0)
    def _():
        load_w.start()

    xn = _rms(x_ref[...], g_ref[...])
    h = xn * (1.0 + _expand_rows(sc_ref, i, tm, seq_len)) + _expand_rows(sh_ref, i, tm, seq_len)
    hb = h.astype(BF16)
    a_low = lax.dot_general(hb, wlow_ref[...].astype(BF16), NT_DIMS, preferred_element_type=F32)
    z = _dot(a_low.astype(BF16), wal_ref[...].astype(BF16)) + bal_ref[...]
    la_ref[...] = (jnp.minimum(z, 0.0) - jnp.log(1.0 + jnp.exp(-jnp.abs(z)))) * (1.0 / GLA_GATE_NORM)

    @pl.when(i == 0)
    def _():
        load_w.wait()

    for jn in range(u_ref.shape[1] // IN_TN):
        cols = slice(jn * IN_TN, (jn + 1) * IN_TN)
        u_ref[:, cols] = lax.dot_general(hb, w_vmem[cols, :], NT_DIMS, preferred_element_type=F32)


def _in_proj(x2, sh, sc, g, w_main_bf, w_low, w_alpha, b_alpha, seq_len):
    n, d = x2.shape
    n_main = w_main_bf.shape[0]
    n_seq = sh.shape[0]
    qk = w_alpha.shape[1]
    rank = w_low.shape[0]
    tm = _tile(n, 256)
    vmem = d * n_main * 2 + 2 * tm * d * 4 + 2 * tm * n_main * 4 + 2 * tm * qk * 4 + 4 * tm * d * 4
    fix = lambda i: (0, 0)
    return pl.pallas_call(
        functools.partial(_in_kernel, tm=tm, seq_len=seq_len),
        grid=(n // tm,),
        in_specs=[pl.BlockSpec((tm, d), lambda i: (i, 0)),
                  _table_spec(n_seq, d, tm, seq_len),
                  _table_spec(n_seq, d, tm, seq_len),
                  pl.BlockSpec((1, d), fix),
                  pl.BlockSpec(memory_space=pl.ANY),
                  pl.BlockSpec((rank, d), fix),
                  pl.BlockSpec((rank, qk), fix),
                  pl.BlockSpec((1, qk), fix)],
        out_specs=[pl.BlockSpec((tm, n_main), lambda i: (i, 0)),
                   pl.BlockSpec((tm, qk), lambda i: (i, 0))],
        out_shape=[jax.ShapeDtypeStruct((n, n_main), F32), jax.ShapeDtypeStruct((n, qk), F32)],
        scratch_shapes=[pltpu.VMEM((n_main, d), BF16), pltpu.SemaphoreType.DMA],
        compiler_params=_params(("arbitrary",), vmem),
        name="in_proj",
    )(x2, sh, sc, g.reshape(1, d), w_main_bf, w_low, w_alpha, b_alpha.reshape(1, qk))


CONV_HALO = 32


def _ln_silu(y, g, b):
    mu = jnp.mean(y, axis=-1, keepdims=True)
    yc = y - mu
    var = jnp.mean(yc * yc, axis=-1, keepdims=True)
    yn = yc * lax.rsqrt(var + EPS) * g + b
    return yn * _sigmoid(yn)


def _conv_seq_kernel(ua_ref, ub_ref, w_ref, b_ref, lg_ref, lb_ref, y_ref, nb_ref, buf, shifted, *, tt, kw):
    t = pl.program_id(1)
    cc = buf.shape[1]

    @pl.when(t == 0)
    def _():
        buf[0:CONV_HALO, :] = jnp.zeros((CONV_HALO, cc), F32)

    buf[CONV_HALO:CONV_HALO + tt, :] = ua_ref[...] * _sigmoid(ub_ref[...])
    span = shifted.shape[1]
    for r in range(1, 8):
        shifted[r - 1] = buf[r:r + span, :]
    first = CONV_HALO - (kw - 1)

    acc = jnp.broadcast_to(b_ref[...], (tt, cc))
    for j in range(kw):
        q, r = divmod(first + j, 8)
        win = buf[8 * q:8 * q + tt, :] if r == 0 else shifted[r - 1, 8 * q:8 * q + tt, :]
        acc = acc + w_ref[j:j + 1, :] * win
    y_ref[...] = _ln_silu(acc, lg_ref[...], lb_ref[...]).astype(y_ref.dtype)

    @pl.when(t == pl.num_programs(1) - 1)
    def _():
        nb_ref[...] = buf[CONV_HALO + tt - (kw - 1):CONV_HALO + tt, :]

    buf[0:CONV_HALO, :] = buf[tt:tt + CONV_HALO, :]


def _conv_seq(u2, n_seq, seq_len, w_dw, b_dw, ln_g, ln_b):
    kw, cc = w_dw.shape
    assert kw - 1 <= CONV_HALO
    tt = _tile(seq_len, 256)
    assert tt % 8 == 0 and tt >= CONV_HALO
    nt = seq_len // tt
    span = tt + CONV_HALO - 8
    row = lambda v: v.reshape(1, cc)
    vmem = 4 * tt * cc * 4 + 2 * tt * cc * 2 + (CONV_HALO + tt + 7 * span) * cc * 4 + 6 * tt * cc * 4
    return pl.pallas_call(
        functools.partial(_conv_seq_kernel, tt=tt, kw=kw),
        grid=(n_seq, nt),
        in_specs=[pl.BlockSpec((tt, cc), lambda b, t: (b * nt + t, 0)),
                  pl.BlockSpec((tt, cc), lambda b, t: (b * nt + t, 1)),
                  pl.BlockSpec((kw, cc), lambda b, t: (0, 0)),
                  pl.BlockSpec((1, cc), lambda b, t: (0, 0)),
                  pl.BlockSpec((1, cc), lambda b, t: (0, 0)),
                  pl.BlockSpec((1, cc), lambda b, t: (0, 0))],
        out_specs=[pl.BlockSpec((tt, cc), lambda b, t: (b * nt + t, 0)),
                   pl.BlockSpec((None, None, kw - 1, cc), lambda b, t: (0, b, 0, 0))],
        out_shape=[jax.ShapeDtypeStruct((n_seq * seq_len, cc), BF16),
                   jax.ShapeDtypeStruct((1, n_seq, kw - 1, cc), F32)],
        scratch_shapes=[pltpu.VMEM((CONV_HALO + tt, cc), F32), pltpu.VMEM((7, span, cc), F32)],
        compiler_params=_params(("arbitrary", "arbitrary"), vmem),
        name="conv_seq",
    )(u2, u2, w_dw, row(b_dw), row(ln_g), row(ln_b))


def _conv_step_kernel(ua_ref, ub_ref, st_ref, w_ref, b_ref, lg_ref, lb_ref, y_ref, nb_ref, gbuf, ybuf,
                      *, kw, t_len):
    hist = kw - 1
    n_seq, cc = st_ref.shape[1], st_ref.shape[2]
    n_lt = gbuf.shape[0]
    lanes = gbuf.shape[2]
    glu = ua_ref[...] * _sigmoid(ub_ref[...])
    for lt in range(n_lt):
        gbuf[lt] = glu[:, lt * lanes:(lt + 1) * lanes]

    def slab(r):
        if r < hist:
            return st_ref[r]
        step = pl.ds(r - hist, n_seq, stride=t_len)
        return jnp.concatenate([gbuf[lt, step, :] for lt in range(n_lt)], axis=1)

    for t in range(t_len):
        acc = jnp.broadcast_to(b_ref[...], (n_seq, cc))
        for j in range(kw):
            acc = acc + w_ref[j:j + 1, :] * slab(t + j)
        y = _ln_silu(acc, lg_ref[...], lb_ref[...])
        for lt in range(n_lt):
            ybuf[lt, pl.ds(t, n_seq, stride=t_len), :] = y[:, lt * lanes:(lt + 1) * lanes]
    y_ref[...] = jnp.concatenate([ybuf[lt] for lt in range(n_lt)], axis=1)
    for r in range(hist):
        nb_ref[r] = slab(r + t_len)


def _conv_step(u2, state_t, t_len, w_dw, b_dw, ln_g, ln_b):
    kw, cc = w_dw.shape
    n_seq = state_t.shape[2]
    bs = _tile(n_seq, 16)
    tb = bs * t_len
    row = lambda v: v.reshape(1, cc)
    vmem = 4 * tb * cc * 4 + 2 * tb * cc * 4 + 4 * (kw - 1) * bs * cc * 4 + tb * cc * 4 + 4 * tb * cc * 4
    hist_spec = pl.BlockSpec((None, kw - 1, bs, cc), lambda i: (0, 0, i, 0))
    return pl.pallas_call(
        functools.partial(_conv_step_kernel, kw=kw, t_len=t_len),
        grid=(n_seq // bs,),
        in_specs=[pl.BlockSpec((tb, cc), lambda i: (i, 0)),
                  pl.BlockSpec((tb, cc), lambda i: (i, 1)),
                  hist_spec,
                  pl.BlockSpec((kw, cc), lambda i: (0, 0)),
                  pl.BlockSpec((1, cc), lambda i: (0, 0)),
                  pl.BlockSpec((1, cc), lambda i: (0, 0)),
                  pl.BlockSpec((1, cc), lambda i: (0, 0))],
        out_specs=[pl.BlockSpec((tb, cc), lambda i: (i, 0)), hist_spec],
        out_shape=[jax.ShapeDtypeStruct((n_seq * t_len, cc), F32),
                   jax.ShapeDtypeStruct(state_t.shape, F32)],
        scratch_shapes=[pltpu.VMEM((cc // V7X_LANES, tb, V7X_LANES), F32),
                        pltpu.VMEM((cc // V7X_LANES, tb, V7X_LANES), F32)],
        compiler_params=_params(("arbitrary",), vmem),
        name="conv_step",
    )(u2, u2, state_t, w_dw, row(b_dw), row(ln_g), row(ln_b))


def _gla_consts(c):
    n_lev = int(np.log2(c))
    i = np.arange(c)
    mall = [(i[None, :] <= i[:, None])]
    masks = []
    for lev in range(n_lev):
        s = 1 << lev
        m = (i // (2 * s)) * (2 * s) + s - 1
        mall.append(i[None, :] <= m[:, None])
        same = (i[:, None] // (2 * s)) == (i[None, :] // (2 * s))
        upper = ((i[:, None] // s) % 2) == 1
        lower = ((i[None, :] // s) % 2) == 0
        masks.append(same & upper & lower)
    mall = jnp.asarray(np.concatenate(mall, axis=0).astype(np.float32), dtype=BF16)
    masks = jnp.asarray(np.stack(masks).astype(np.float32))
    return mall, masks, n_lev


def _gla_intra(units, mask_ref, c, n_lev):
    nt_dims = (((1,), (1,)), ((), ()))
    tn_dims = (((0,), (0,)), ((), ()))
    staged = []
    for q, k, v, ball in units:
        bc = ball[0:c]
        b_last = bc[c - 1:c, :]
        levels = []
        for lev in range(n_lev):
            bm = ball[(1 + lev) * c:(2 + lev) * c]
            decay = jnp.exp(-jnp.abs(bc - bm))
            levels.append(((q * decay).astype(BF16), (k * decay).astype(BF16)))
        staged.append(dict(qe=(q * jnp.exp(bc)).astype(BF16), kd=(k * jnp.exp(b_last - bc)).astype(BF16),
                           vb=v.astype(BF16), diag=jnp.sum(q * k, axis=-1, keepdims=True) * v,
                           b_last=b_last, levels=levels))
    atts = []
    for u in staged:
        att = None
        for lev, (ql, kl) in enumerate(u["levels"]):
            a = lax.dot_general(ql, kl, nt_dims, preferred_element_type=F32)
            a = jnp.where(mask_ref[lev] > 0.5, a, 0.0)
            att = a if att is None else att + a
        atts.append(att)
    out = []
    for u, att in zip(staged, atts):
        dk = u["kd"].shape[1]
        dv = u["vb"].shape[1]
        o_intra = _dot(att.astype(BF16), u["vb"]) + u["diag"]
        upd = lax.dot_general(u["kd"], u["vb"], tn_dims, preferred_element_type=F32)
        dec = jnp.exp(jnp.transpose(jnp.broadcast_to(u["b_last"], (dk, dk))))
        out.append((u["qe"], o_intra, upd, jnp.tile(dec, (1, dv // dk))))
    return out


def _gla_finish(o, g, gn):
    return _rms(o, gn) * (g * _sigmoid(g))


GLA_SEQ_CHUNKS = 4
GLA_STEP_SEQS = 4


def _gla_seq_kernel(q_ref, k_ref, v_ref, g_ref, la_ref, gn_ref, mall_ref, mask_ref, y_ref, sf_ref, s_scr,
                    *, c, n_lev, heads):
    t = pl.program_id(1)
    dk = q_ref.shape[1] // heads
    dv = v_ref.shape[1] // heads
    n_ch = q_ref.shape[0] // c

    @pl.when(t == 0)
    def _():
        s_scr[...] = jnp.zeros(s_scr.shape, F32)

    mall = mall_ref[...]
    units = []
    for ci in range(n_ch):
        rows = slice(ci * c, (ci + 1) * c)
        ball = _dot_sel(mall, la_ref[rows, :])
        for h in range(heads):
            ks = slice(h * dk, (h + 1) * dk)
            units.append((q_ref[rows, ks] * (dk ** -0.5), k_ref[rows, ks], v_ref[rows, h * dv:(h + 1) * dv],
                          ball[:, ks]))
    parts = _gla_intra(units, mask_ref, c, n_lev)
    state = [s_scr[h] for h in range(heads)]
    for ci in range(n_ch):
        rows = slice(ci * c, (ci + 1) * c)
        for h in range(heads):
            vs = slice(h * dv, (h + 1) * dv)
            qe, o_intra, upd, dec = parts[ci * heads + h]
            o = o_intra + _dot(qe, state[h].astype(BF16))
            state[h] = state[h] * dec + upd
            y_ref[rows, vs] = _gla_finish(o, g_ref[rows, vs], gn_ref[...]).astype(y_ref.dtype)
    for h in range(heads):
        s_scr[h] = state[h]

    @pl.when(t == pl.num_programs(1) - 1)
    def _():
        sf_ref[...] = s_scr[...]


def _gla_seq(u2, la, n_seq, seq_len, gn_g, cc, qk):
    heads = GLA_HEADS
    dk = qk // heads
    dv = cc // heads
    c = int(np.gcd(seq_len, GLA_CHUNK))
    n_ch = GLA_SEQ_CHUNKS if (seq_len // c) % GLA_SEQ_CHUNKS == 0 else 1
    tb = n_ch * c
    nt = seq_len // tb
    mall, masks, n_lev = _gla_consts(c)
    qb = 2 * cc // qk
    vb = (2 * cc + 2 * qk) // cc
    vmem = 2 * tb * (3 * qk + 3 * cc) * 4 + 3 * heads * dk * dv * 4 + 96 * n_ch * heads * c * max(dv, 128) * 4
    return pl.pallas_call(
        functools.partial(_gla_seq_kernel, c=c, n_lev=n_lev, heads=heads),
        grid=(n_seq, nt),
        in_specs=[pl.BlockSpec((tb, qk), lambda b, t: (b * nt + t, qb)),
                  pl.BlockSpec((tb, qk), lambda b, t: (b * nt + t, qb + 1)),
                  pl.BlockSpec((tb, cc), lambda b, t: (b * nt + t, vb)),
                  pl.BlockSpec((tb, cc), lambda b, t: (b * nt + t, vb + 1)),
                  pl.BlockSpec((tb, qk), lambda b, t: (b * nt + t, 0)),
                  pl.BlockSpec((1, dv), lambda b, t: (0, 0)),
                  pl.BlockSpec(mall.shape, lambda b, t: (0, 0)),
                  pl.BlockSpec(masks.shape, lambda b, t: (0, 0, 0))],
        out_specs=[pl.BlockSpec((tb, cc), lambda b, t: (b * nt + t, 0)),
                   pl.BlockSpec((None, None, heads, dk, dv), lambda b, t: (0, b, 0, 0, 0))],
        out_shape=[jax.ShapeDtypeStruct((n_seq * seq_len, cc), BF16),
                   jax.ShapeDtypeStruct((1, n_seq, heads, dk, dv), F32)],
        scratch_shapes=[pltpu.VMEM((heads, dk, dv), F32)],
        compiler_params=_params(("arbitrary", "arbitrary"), vmem),
        name="gla_seq",
    )(u2, u2, u2, u2, la, gn_g.reshape(1, dv), mall, masks)


def _gla_step_kernel(q_ref, k_ref, v_ref, g_ref, la_ref, st_ref, gn_ref, mall_ref, mask_ref, y_ref, sn_ref,
                     *, c, n_lev, heads):
    dk = q_ref.shape[2] // heads
    dv = v_ref.shape[2] // heads
    mall = mall_ref[...]

    n_un = GLA_STEP_SEQS if q_ref.shape[0] % GLA_STEP_SEQS == 0 else 1

    def body(t, carry):
        units = []
        for ss in range(n_un):
            s = t * n_un + ss
            ball = _dot_sel(mall, la_ref[s])
            for h in range(heads):
                ks = slice(h * dk, (h + 1) * dk)
                units.append((q_ref[s, :, ks] * (dk ** -0.5), k_ref[s, :, ks], v_ref[s, :, h * dv:(h + 1) * dv],
                              ball[:, ks]))
        parts = _gla_intra(units, mask_ref, c, n_lev)
        for ss in range(n_un):
            s = t * n_un + ss
            for h in range(heads):
                vs = slice(h * dv, (h + 1) * dv)
                qe, o_intra, upd, dec = parts[ss * heads + h]
                s0 = st_ref[s, h]
                o = o_intra + _dot(qe, s0.astype(BF16))
                sn_ref[s, h] = s0 * dec + upd
                y_ref[s, :, vs] = _gla_finish(o, g_ref[s, :, vs], gn_ref[...])
        return carry

    lax.fori_loop(0, q_ref.shape[0] // n_un, body, 0)


def _gla_step(u3, la3, state, gn_g, cc, qk):
    n_seq, t_len, _ = u3.shape
    heads = GLA_HEADS
    dk = qk // heads
    dv = cc // heads
    assert t_len <= GLA_CHUNK and GLA_CHUNK % t_len == 0, "single-chunk path"
    mall, masks, n_lev = _gla_consts(t_len)
    bs = min(8, n_seq)
    qb = 2 * cc // qk
    vb = (2 * cc + 2 * qk) // cc
    vmem = 2 * bs * t_len * (3 * qk + 3 * cc) * 4 + 4 * bs * heads * dk * dv * 4 + 64 * 8 * max(dv, 128) * 4
    return pl.pallas_call(
        functools.partial(_gla_step_kernel, c=t_len, n_lev=n_lev, heads=heads),
        grid=(n_seq // bs,),
        in_specs=[pl.BlockSpec((bs, t_len, qk), lambda i: (i, 0, qb)),
                  pl.BlockSpec((bs, t_len, qk), lambda i: (i, 0, qb + 1)),
                  pl.BlockSpec((bs, t_len, cc), lambda i: (i, 0, vb)),
                  pl.BlockSpec((bs, t_len, cc), lambda i: (i, 0, vb + 1)),
                  pl.BlockSpec((bs, t_len, qk), lambda i: (i, 0, 0)),
                  pl.BlockSpec((None, bs, heads, dk, dv), lambda i: (0, i, 0, 0, 0)),
                  pl.BlockSpec((1, dv), lambda i: (0, 0)),
                  pl.BlockSpec(mall.shape, lambda i: (0, 0)),
                  pl.BlockSpec(masks.shape, lambda i: (0, 0, 0))],
        out_specs=[pl.BlockSpec((bs, t_len, cc), lambda i: (i, 0, 0)),
                   pl.BlockSpec((None, bs, heads, dk, dv), lambda i: (0, i, 0, 0, 0))],
        out_shape=[jax.ShapeDtypeStruct((n_seq, t_len, cc), F32),
                   jax.ShapeDtypeStruct((1, n_seq, heads, dk, dv), F32)],
        compiler_params=_params(("arbitrary",), vmem),
        name="gla_step",
    )(u3, u3, u3, u3, la3, state, gn_g.reshape(1, dv), mall, masks)


def _out_kernel(x_ref, yc_ref, yg_ref, wo_ref, g1_ref, sh_ref, sc_ref, n2_ref, wr_ref, br_ref,
                x1_ref, h2_ref, ti_ref, tw_ref, *, tm, seq_len):
    i = pl.program_id(0)
    cc = yc_ref.shape[1]
    n_exp = wr_ref.shape[1]
    mix = _dot(yc_ref[...].astype(BF16), wo_ref[0:cc, :]) + _dot(yg_ref[...].astype(BF16), wo_ref[cc:, :])
    x1 = x_ref[...] + _expand_rows(g1_ref, i, tm, seq_len) * mix
    x1_ref[...] = x1
    h2 = _rms(x1, n2_ref[...]) * (1.0 + _expand_rows(sc_ref, i, tm, seq_len)) + _expand_rows(sh_ref, i, tm, seq_len)
    h2_ref[...] = h2

    logits = _dot(h2.astype(BF16), wr_ref[...].astype(BF16)) + br_ref[...]
    lane = lax.broadcasted_iota(I32, logits.shape, 1)
    vals = []
    for r in range(TOP_K):
        m = jnp.max(logits, axis=-1, keepdims=True)
        idx = jnp.min(jnp.where(logits == m, lane, n_exp), axis=-1, keepdims=True)
        vals.append(m)
        ti_ref[:, r:r + 1] = idx
        logits = jnp.where(lane == idx, -jnp.inf, logits)
    ex = [jnp.exp(v - vals[0]) for v in vals]
    den = ex[0] + ex[1] + ex[2] + ex[3]
    for r in range(TOP_K):
        tw_ref[:, r:r + 1] = ex[r] / den


def _out_proj(x2, yc, yg, wo_bf, g1, sh2, sc2, n2, w_router, b_router, seq_len):
    n, d = x2.shape
    cc = yc.shape[1]
    n_seq = g1.shape[0]
    n_exp = w_router.shape[1]
    tm = _tile(n, 256)
    vmem = (6 * tm * d * 4 + 4 * tm * cc * 4 + 2 * 2 * cc * d * 2 + 8 * tm * d * 4)
    tok = lambda i: (i, 0)
    fix = lambda i: (0, 0)
    return pl.pallas_call(
        functools.partial(_out_kernel, tm=tm, seq_len=seq_len),
        grid=(n // tm,),
        in_specs=[pl.BlockSpec((tm, d), tok),
                  pl.BlockSpec((tm, cc), tok),
                  pl.BlockSpec((tm, cc), tok),
                  pl.BlockSpec((2 * cc, d), fix),
                  _table_spec(n_seq, d, tm, seq_len),
                  _table_spec(n_seq, d, tm, seq_len),
                  _table_spec(n_seq, d, tm, seq_len),
                  pl.BlockSpec((1, d), fix),
                  pl.BlockSpec((d, n_exp), fix),
                  pl.BlockSpec((1, n_exp), fix)],
        out_specs=[pl.BlockSpec((tm, d), tok),
                   pl.BlockSpec((tm, d), tok),
                   pl.BlockSpec((tm, TOP_K), tok),
                   pl.BlockSpec((tm, TOP_K), tok)],
        out_shape=[jax.ShapeDtypeStruct((n, d), F32),
                   jax.ShapeDtypeStruct((n, d), F32),
                   jax.ShapeDtypeStruct((n, TOP_K), I32),
                   jax.ShapeDtypeStruct((n, TOP_K), F32)],
        compiler_params=_params(("arbitrary",), vmem),
        name="out_proj",
    )(x2, yc, yg, wo_bf, g1, sh2, sc2, n2.reshape(1, d), w_router, b_router.reshape(1, n_exp))


def _rank_kernel(ti_ref, slot_ref, cnt_ref, off_ref, carry, offs, *, tr, n_exp, group):
    p = pl.program_id(0)
    i = pl.program_id(1)
    lane = lax.broadcasted_iota(I32, (tr, n_exp), 1)
    ti = ti_ref[...]
    hot = jnp.zeros((tr, n_exp), F32)
    for k in range(TOP_K):
        hot = hot + (lane == ti[:, k:k + 1]).astype(F32)

    @pl.when((p == 0) & (i == 0))
    def _():
        carry[...] = jnp.zeros(carry.shape, F32)

    @pl.when((p == 1) & (i == 0))
    def _():
        cnt = carry[0:1, :]
        padded = jnp.ceil(cnt * (1.0 / group)) * group
        r = lax.broadcasted_iota(I32, (n_exp, n_exp), 0)
        c = lax.broadcasted_iota(I32, (n_exp, n_exp), 1)
        before = (r < c).astype(BF16)
        hi, mid, lo = _split3(jnp.broadcast_to(padded, (8, n_exp)))
        off = _dot(hi, before) + _dot(mid, before) + _dot(lo, before)
        offs[...] = off
        cnt_ref[...] = cnt.astype(I32)
        off_ref[...] = off[0:1, :].astype(I32)
        carry[...] = jnp.zeros(carry.shape, F32)

    @pl.when(p == 1)
    def _():
        r = lax.broadcasted_iota(I32, (tr, tr), 0)
        c = lax.broadcasted_iota(I32, (tr, tr), 1)
        earlier = (c < r).astype(BF16)
        tot = _dot(earlier, hot.astype(BF16)) + carry[0:1, :] + offs[0:1, :]
        for k in range(TOP_K):
            sel = jnp.where(lane == ti[:, k:k + 1], tot, 0.0)
            slot_ref[:, k:k + 1] = jnp.sum(sel, axis=-1, keepdims=True).astype(I32)

    carry[...] = carry[...] + jnp.sum(hot, axis=0, keepdims=True)


def _rank(top_i, n_exp, group):
    n = top_i.shape[0]
    tr = _tile(n, 512)
    return pl.pallas_call(
        functools.partial(_rank_kernel, tr=tr, n_exp=n_exp, group=group),
        grid=(2, n // tr),
        in_specs=[pl.BlockSpec((tr, TOP_K), lambda p, i: (i, 0))],
        out_specs=[pl.BlockSpec((tr, TOP_K), lambda p, i: (i * p, 0)),
                   pl.BlockSpec((1, n_exp), lambda p, i: (0, 0)),
                   pl.BlockSpec((1, n_exp), lambda p, i: (0, 0))],
        out_shape=[jax.ShapeDtypeStruct((n, TOP_K), I32),
                   jax.ShapeDtypeStruct((1, n_exp), I32),
                   jax.ShapeDtypeStruct((1, n_exp), I32)],
        scratch_shapes=[pltpu.VMEM((8, n_exp), F32), pltpu.VMEM((8, n_exp), F32)],
        compiler_params=_params(("arbitrary", "arbitrary"), 8 * tr * tr * 4),
        name="rank",
    )(top_i)


def _scatter_rows(slot_ref, h_ref, xs_ref, sem, td):
    def issue(r, carry):
        for k in range(TOP_K):
            s = slot_ref[0, r * TOP_K + k]
            pltpu.make_async_copy(h_ref.at[pl.ds(r, 1), :], xs_ref.at[pl.ds(s, 1), :], sem).start()
        return carry

    lax.fori_loop(0, td, issue, 0)
    for k in range(TOP_K):
        pltpu.make_async_copy(h_ref, xs_ref.at[pl.ds(0, td), :], sem).wait()


def _dispatch_kernel(cnt, off, slot_ref, ha_ref, hb_ref, xs_ref, zbuf, sem, zsem, *, td, group, n_a):
    n_exp = cnt.shape[0]
    n_rows = xs_ref.shape[0]
    i = pl.program_id(0)

    def padded(e):
        return (cnt[e] + group - 1) // group * group

    def z_row(row):
        return pltpu.make_async_copy(zbuf.at[pl.ds(0, 1), :], xs_ref.at[pl.ds(row, 1), :], zsem)

    def z_block(row):
        return pltpu.make_async_copy(zbuf, xs_ref.at[pl.ds(pl.multiple_of(row, group), group), :], zsem)

    def for_pad_rows(fn):
        def expert(e, carry):
            lax.fori_loop(off[e] + cnt[e], off[e] + padded(e), lambda r, c: (fn(r), c)[1], 0)
            return carry
        lax.fori_loop(0, n_exp, expert, 0)

    def for_tail_blocks(fn):
        first = (off[n_exp - 1] + padded(n_exp - 1)) // group
        lax.fori_loop(first, n_rows // group, lambda b, c: (fn(b * group), c)[1], 0)

    @pl.when(i == 0)
    def _():
        zbuf[...] = jnp.zeros(zbuf.shape, zbuf.dtype)
        for_pad_rows(lambda r: z_row(r).start())
        for_tail_blocks(lambda r: z_block(r).start())

    @pl.when(i < n_a)
    def _():
        _scatter_rows(slot_ref, ha_ref, xs_ref, sem, td)

    @pl.when(i >= n_a)
    def _():
        _scatter_rows(slot_ref, hb_ref, xs_ref, sem, td)

    @pl.when(i == pl.num_programs(0) - 1)
    def _():
        for_pad_rows(lambda r: z_row(0).wait())
        for_tail_blocks(lambda r: z_block(0).wait())


def _dispatch(slots, h_a, h_b, cnt, off, n_rows, group):
    d = h_a.shape[1]
    td = _tile(int(np.gcd(h_a.shape[0], h_b.shape[0])), 256)
    n_a, n_b = h_a.shape[0] // td, h_b.shape[0] // td
    grid_spec = pltpu.PrefetchScalarGridSpec(
        num_scalar_prefetch=2,
        grid=(n_a + n_b,),
        in_specs=[pl.BlockSpec((None, 1, td * TOP_K), lambda i, *_: (i, 0, 0), memory_space=pltpu.SMEM),
                  pl.BlockSpec((td, d), lambda i, *_: (jnp.minimum(i, n_a - 1), 0)),
                  pl.BlockSpec((td, d), lambda i, *_: (jnp.maximum(i - n_a, 0), 0))],
        out_specs=pl.BlockSpec(memory_space=pl.ANY),
        scratch_shapes=[pltpu.VMEM((group, d), h_a.dtype), pltpu.SemaphoreType.DMA, pltpu.SemaphoreType.DMA])
    return pl.pallas_call(
        functools.partial(_dispatch_kernel, td=td, group=group, n_a=n_a),
        grid_spec=grid_spec,
        out_shape=jax.ShapeDtypeStruct((n_rows, d), h_a.dtype),
        compiler_params=_params(("arbitrary",), 4 * td * d * 4 + group * d * 4),
        name="dispatch",
    )(cnt, off, slots.reshape(n_a + n_b, 1, td * TOP_K), h_a, h_b)


MOE_CHUNK = 1536
MOE_SUB = 128
MOE_MM = 2
MOE_TF = 256
MOE_TN = 512


def _for_row_blocks(n_units, body):
    n = n_units // MOE_MM

    def block(b):
        body(b * MOE_MM, MOE_MM)

    def trip(t, carry):
        for u in range(4):
            block(t * 4 + u)
        return carry

    lax.fori_loop(0, n // 4, trip, 0)

    @pl.when((n // 2) % 2 == 1)
    def _():
        block((n // 4) * 4)
        block((n // 4) * 4 + 1)

    @pl.when(n % 2 == 1)
    def _():
        block(n - 1)

    @pl.when(n_units % MOE_MM == 1)
    def _():
        body(n_units - 1, 1)


def _moe_kernel(che, chrow, chn, tail, xs_ref, wg_ref, wu_ref, bg_ref, bu_ref, wd_ref, bd_ref, yb_ref,
                xbuf, xb, act, wgb, wub, wdb, obuf, semx, semo, *, n_f, n_n, n_chunks):
    c = pl.program_id(0)
    j = pl.program_id(1)
    nsb = chn[c]
    sub = MOE_SUB

    def x_copy(cc, i):
        row = pl.multiple_of(chrow[cc] + i * sub, sub)
        return pltpu.make_async_copy(xs_ref.at[pl.ds(row, sub), :],
                                     xbuf.at[pl.ds(pl.multiple_of(i * sub, sub), sub), :], semx)

    def x_fetch(cc):
        lax.fori_loop(0, chn[cc], lambda i, carry: (x_copy(cc, i).start(), carry)[1], 0)

    def o_copy(slot, i, jj):
        row = pl.multiple_of(chrow[c] + i * sub, sub)
        col = pl.multiple_of(jj * MOE_TN, MOE_TN)
        return pltpu.make_async_copy(obuf.at[slot, pl.ds(pl.multiple_of(i * sub, sub), sub), :],
                                     yb_ref.at[pl.ds(row, sub), pl.ds(col, MOE_TN)], semo.at[slot])

    @pl.when((c == 0) & (j == 0))
    def _():
        x_fetch(0)

    @pl.when(j == 0)
    def _():
        lax.fori_loop(0, nsb, lambda i, carry: (x_copy(c, 0).wait(), carry)[1], 0)

        def cast(i, carry):
            rows = pl.ds(pl.multiple_of(i * sub, sub), sub)
            xb[rows, :] = xbuf[rows, :].astype(BF16)
            return carry

        lax.fori_loop(0, nsb, cast, 0)

    @pl.when((j == 1) & (c + 1 < n_chunks))
    def _():
        x_fetch(jnp.minimum(c + 1, n_chunks - 1))

    @pl.when((j < n_f) & (nsb > 0))
    def _():
        wgb[...] = wg_ref[...].astype(BF16)
        wub[...] = wu_ref[...].astype(BF16)
        tile_row = pl.ds(che[c] * n_f + j, 1)
        bg = bg_ref[tile_row, :]
        bu = bu_ref[tile_row, :]

        def row_block(u, n):
            rows = pl.ds(pl.multiple_of(u * sub, sub), n * sub)
            x = xb[rows, :]
            gate = jnp.minimum(_dot(x, wgb[...]) + bg, SWIGLU_LIMIT)
            up = jnp.clip(_dot(x, wub[...]) + bu, -SWIGLU_LIMIT, SWIGLU_LIMIT)
            glu = gate * _sigmoid(SWIGLU_ALPHA * gate)
            act[j, rows, :] = ((up + 1.0) * glu).astype(BF16)

        _for_row_blocks(nsb, row_block)

    @pl.when((j >= n_f) & (nsb > 0))
    def _():
        jj = j - n_f
        slot = jj % 2
        wdb[...] = wd_ref[...].astype(BF16)
        bd = bd_ref[pl.ds(che[c] * n_n + jj, 1), :]

        def retire(s):
            def one(i, carry):
                o_copy(s, 0, 0).wait()
                return carry
            lax.fori_loop(0, nsb, one, 0)

        @pl.when(jj >= 2)
        def _():
            retire(slot)

        def row_block(u, n):
            rows = pl.ds(pl.multiple_of(u * sub, sub), n * sub)
            a = jnp.concatenate([act[f, rows, :] for f in range(n_f)], axis=1)
            obuf[slot, rows, :] = _dot(a, wdb[...]) + bd
            for k in range(n):
                o_copy(slot, u + k, jj).start()

        _for_row_blocks(nsb, row_block)

        @pl.when(jj == n_n - 1)
        def _():
            retire(slot)
            if n_n >= 2:
                retire(1 - slot)

    @pl.when((c == n_chunks - 1) & (j == n_f + n_n - 1))
    def _():
        n_tail = (yb_ref.shape[0] - tail[0]) // sub
        obuf[0, 0:sub, :] = jnp.zeros((sub, MOE_TN), F32)

        def z_copy(i, js):
            row = pl.multiple_of(tail[0] + i * sub, sub)
            return pltpu.make_async_copy(obuf.at[0, pl.ds(0, sub), :],
                                         yb_ref.at[pl.ds(row, sub), pl.ds(js * MOE_TN, MOE_TN)], semo.at[0])

        def fill(i, carry):
            for js in range(n_n):
                z_copy(i, js).start()
            return carry

        def drain(i, carry):
            for js in range(n_n):
                z_copy(0, js).wait()
            return carry

        lax.fori_loop(0, n_tail, fill, 0)
        lax.fori_loop(0, n_tail, drain, 0)


def _moe(xs, yb_rows, ch_e, ch_row, ch_n, tail, w_gate, b_gate, w_up, b_up, w_down, b_down):
    n_exp, d, f = w_gate.shape
    n_f = f // MOE_TF
    n_n = d // MOE_TN
    n_chunks = ch_e.shape[0]

    def w1(c, j, che, chrow, chn, tail):
        return (che[c], 0, jnp.where(chn[c] > 0, jnp.minimum(j, n_f - 1), n_f - 1))

    def w2(c, j, che, chrow, chn, tail):
        return (che[c], 0, jnp.where(chn[c] > 0, jnp.maximum(j - n_f, 0), n_n - 1))

    def fixed(c, j, che, chrow, chn, tail):
        return (0, 0)

    vmem = (MOE_CHUNK * d * 4 + MOE_CHUNK * d * 2 + MOE_CHUNK * f * 2
            + 2 * 2 * d * MOE_TF * 4 + 2 * f * MOE_TN * 4 + 2 * d * MOE_TF * 2 + f * MOE_TN * 2
            + 2 * MOE_CHUNK * MOE_TN * 4 + 8 * MOE_MM * MOE_SUB * max(d, f) * 4)
    grid_spec = pltpu.PrefetchScalarGridSpec(
        num_scalar_prefetch=4,
        grid=(n_chunks, n_f + n_n),
        in_specs=[pl.BlockSpec(memory_space=pl.ANY),
                  pl.BlockSpec((None, d, MOE_TF), w1),
                  pl.BlockSpec((None, d, MOE_TF), w1),
                  pl.BlockSpec((n_exp * n_f, MOE_TF), fixed),
                  pl.BlockSpec((n_exp * n_f, MOE_TF), fixed),
                  pl.BlockSpec((None, f, MOE_TN), w2),
                  pl.BlockSpec((n_exp * n_n, MOE_TN), fixed)],
        out_specs=pl.BlockSpec(memory_space=pl.ANY),
        scratch_shapes=[pltpu.VMEM((MOE_CHUNK, d), F32),
                        pltpu.VMEM((MOE_CHUNK, d), BF16),
                        pltpu.VMEM((n_f, MOE_CHUNK, MOE_TF), BF16),
                        pltpu.VMEM((d, MOE_TF), BF16),
                        pltpu.VMEM((d, MOE_TF), BF16),
                        pltpu.VMEM((f, MOE_TN), BF16),
                        pltpu.VMEM((2, MOE_CHUNK, MOE_TN), F32),
                        pltpu.SemaphoreType.DMA,
                        pltpu.SemaphoreType.DMA((2,))])
    return pl.pallas_call(
        functools.partial(_moe_kernel, n_f=n_f, n_n=n_n, n_chunks=n_chunks),
        grid_spec=grid_spec,
        out_shape=jax.ShapeDtypeStruct((yb_rows, d), F32),
        compiler_params=_params(("arbitrary", "arbitrary"), vmem),
        name="moe",
    )(ch_e, ch_row, ch_n, tail, xs, w_gate, w_up, b_gate.reshape(n_exp * n_f, MOE_TF),
      b_up.reshape(n_exp * n_f, MOE_TF), w_down, b_down.reshape(n_exp * n_n, MOE_TN))


def _chunk_table(cnt, off, n_chunks):
    n_exp = cnt.shape[0]
    per = MOE_CHUNK // MOE_SUB
    nsb = (cnt + MOE_SUB - 1) // MOE_SUB
    nch = (nsb + per - 1) // per
    end = jnp.cumsum(nch)
    start = end - nch
    cidx = jnp.arange(n_chunks, dtype=I32)
    owner = lambda ci: jnp.minimum(jnp.sum((end[None, :] <= ci[:, None]).astype(I32), axis=1), n_exp - 1)
    e = owner(cidx)
    used = cidx < end[-1]
    last_e = owner(end[-1:] - 1)[0]
    local = cidx - start[e]
    ch_e = jnp.where(used, e, last_e)
    ch_row = jnp.where(used, off[e] + local * MOE_CHUNK, 0).astype(I32)
    ch_n = jnp.where(used, jnp.minimum(per, nsb[e] - local * per), 0).astype(I32)
    tail = (off[-1] + nsb[-1] * MOE_SUB).astype(I32).reshape(1)
    return ch_e, ch_row, ch_n, tail


def _combine_kernel(slot_ref, next_slot_ref, tw_ref, x1_ref, g2_ref, gf_ref, yb_ref, y_ref, gbuf, sem,
                    *, tc, seq_len):
    i = pl.program_id(0)
    cur = i % 2

    def gather(s_ref, half):
        def rows(r, carry):
            for k in range(TOP_K):
                s = s_ref[0, r * TOP_K + k]
                pltpu.make_async_copy(yb_ref.at[pl.ds(s, 1), :], gbuf.at[half, k, pl.ds(r, 1), :],
                                      sem.at[half]).start()
            return carry
        lax.fori_loop(0, tc, rows, 0)

    @pl.when(i == 0)
    def _():
        gather(slot_ref, 0)

    @pl.when(i + 1 < pl.num_programs(0))
    def _():
        gather(next_slot_ref, 1 - cur)

    for k in range(TOP_K):
        pltpu.make_async_copy(yb_ref.at[pl.ds(0, tc), :], gbuf.at[cur, k], sem.at[cur]).wait()
    tw = tw_ref[...]
    y = tw[:, 0:1] * gbuf[cur, 0]
    for k in range(1, TOP_K):
        y = y + tw[:, k:k + 1] * gbuf[cur, k]
    x2 = x1_ref[...] + _expand_rows(g2_ref, i, tc, seq_len) * y
    y_ref[...] = _rms(x2, gf_ref[...])


def _combine(slots, top_w, x1, g2, gf, yb, seq_len):
    n, d = x1.shape
    n_seq = g2.shape[0]
    tc = _tile(n, 128)
    nt = n // tc
    vmem = 2 * TOP_K * tc * d * 4 + 4 * tc * d * 4 + 6 * tc * d * 4
    slots3 = slots.reshape(nt, 1, tc * TOP_K)
    return pl.pallas_call(
        functools.partial(_combine_kernel, tc=tc, seq_len=seq_len),
        grid=(nt,),
        in_specs=[pl.BlockSpec((None, 1, tc * TOP_K), lambda i: (i, 0, 0), memory_space=pltpu.SMEM),
                  pl.BlockSpec((None, 1, tc * TOP_K), lambda i: (jnp.minimum(i + 1, nt - 1), 0, 0),
                               memory_space=pltpu.SMEM),
                  pl.BlockSpec((tc, TOP_K), lambda i: (i, 0)),
                  pl.BlockSpec((tc, d), lambda i: (i, 0)),
                  _table_spec(n_seq, d, tc, seq_len),
                  pl.BlockSpec((1, d), lambda i: (0, 0)),
                  pl.BlockSpec(memory_space=pl.ANY)],
        out_specs=pl.BlockSpec((tc, d), lambda i: (i, 0)),
        out_shape=jax.ShapeDtypeStruct((n, d), F32),
        scratch_shapes=[pltpu.VMEM((2, TOP_K, tc, d), F32), pltpu.SemaphoreType.DMA((2,))],
        compiler_params=_params(("arbitrary",), vmem),
        name="combine",
    )(slots3, slots3, top_w, x1, g2, gf.reshape(1, d), yb)


def _layer(xp, xs, cp, cs, conv_state, gla_state, w):
    (w_ada, b_ada, norm1_g, w_in, w_dw, b_dw, conv_ln_g, conv_ln_b, w_alpha, b_alpha, gla_norm_g, w_out,
     norm2_g, w_router, b_router, w_gate, b_gate, w_up, b_up, w_down, b_down, final_g) = w
    bp, tp, d = xp.shape
    bs, ts, _ = xs.shape
    cc = w_dw.shape[1]
    qk = w_alpha.shape[1]
    n_exp = w_router.shape[1]
    n_main = 2 * cc + 2 * qk + 2 * cc
    np_, ns_ = bp * tp, bs * ts

    mod = _ada(jnp.concatenate([cp, cs], axis=0), w_ada, b_ada)
    mods_p = [mod[:bp, i * d:(i + 1) * d] for i in range(6)]
    mods_s = [mod[bp:, i * d:(i + 1) * d] for i in range(6)]
    w_in_t = w_in.T
    w_low = w_in_t[n_main:]
    w_main_bf = w_in_t[:n_main].astype(BF16)
    wo_bf = w_out.astype(BF16)

    xp2 = xp.reshape(np_, d)
    xs2 = xs.reshape(ns_, d)

    u_p, la_p = _in_proj(xp2, mods_p[0], mods_p[1], norm1_g, w_main_bf, w_low, w_alpha, b_alpha, tp)
    yc_p, conv_p = _conv_seq(u_p, bp, tp, w_dw, b_dw, conv_ln_g, conv_ln_b)
    yg_p, gla_p = _gla_seq(u_p, la_p, bp, tp, gla_norm_g, cc, qk)
    x1_p, h2_p, ti_p, tw_p = _out_proj(xp2, yc_p, yg_p, wo_bf, mods_p[2], mods_p[3], mods_p[4], norm2_g,
                                       w_router, b_router, tp)

    u_s, la_s = _in_proj(xs2, mods_s[0], mods_s[1], norm1_g, w_main_bf, w_low, w_alpha, b_alpha, ts)
    yc_s, conv_s = _conv_step(u_s, jnp.transpose(conv_state, (0, 2, 1, 3)), ts, w_dw, b_dw, conv_ln_g, conv_ln_b)
    conv_s = jnp.transpose(conv_s, (0, 2, 1, 3))
    yg_s, gla_s = _gla_step(u_s.reshape(bs, ts, n_main), la_s.reshape(bs, ts, qk), gla_state, gla_norm_g, cc, qk)
    x1_s, h2_s, ti_s, tw_s = _out_proj(xs2, yc_s, yg_s.reshape(ns_, cc), wo_bf, mods_s[2],
                                       mods_s[3], mods_s[4], norm2_g, w_router, b_router, ts)

    n_asg = (np_ + ns_) * TOP_K
    slots, cnt, off = _rank(jnp.concatenate([ti_p, ti_s], axis=0), n_exp, MOE_SUB)
    rows = -(-(n_asg + n_exp * MOE_SUB) // MOE_SUB) * MOE_SUB
    xsort = _dispatch(slots, h2_p, h2_s, cnt[0], off[0], rows, MOE_SUB)
    n_chunks = (n_asg + n_exp * MOE_SUB) // MOE_CHUNK + n_exp
    ch_e, ch_row, ch_n, tail = _chunk_table(cnt[0], off[0], n_chunks)
    yb = _moe(xsort, rows, ch_e, ch_row, ch_n, tail, w_gate, b_gate, w_up, b_up, w_down, b_down)

    y_p = _combine(slots[:np_], tw_p, x1_p, mods_p[5], final_g, yb, tp)
    y_s = _combine(slots[np_:], tw_s, x1_s, mods_s[5], final_g, yb, ts)
    return y_p.reshape(bp, tp, d), y_s.reshape(bs, ts, d), conv_p, gla_p, conv_s, gla_s


def kernel(x_prompt, x_sample, c_prompt, c_sample, state_conv, state_gla, w_ada, b_ada, norm1_g, w_in, w_dw, b_dw,
           conv_ln_g, conv_ln_b, w_alpha, b_alpha, gla_norm_g, w_out, norm2_g, w_router, b_router, w_gate, b_gate,
           w_up, b_up, w_down, b_down, final_norm_g):
    assert w_ada.shape[0] == 1, "the final norm is fused into the (single) layer"
    first = lambda a: a.reshape(a.shape[1:])
    w = tuple(first(a) for a in (w_ada, b_ada, norm1_g, w_in, w_dw, b_dw, conv_ln_g, conv_ln_b, w_alpha, b_alpha,
                                 gla_norm_g, w_out, norm2_g, w_router, b_router, w_gate, b_gate, w_up, b_up,
                                 w_down, b_down)) + (final_norm_g,)
    return _layer(x_prompt, x_sample, c_prompt, c_sample, state_conv, state_gla, w)
```

```python
import functools

import numpy as np
import jax
import jax.numpy as jnp
from jax import lax
from jax.experimental import pallas as pl
from jax.experimental.pallas import tpu as pltpu

F32 = jnp.float32
BF16 = jnp.bfloat16
I32 = jnp.int32

EPS = 1e-5
GLA_HEADS = 4
GLA_GATE_NORM = 16.0
GLA_CHUNK = 64
TOP_K = 4
SWIGLU_LIMIT = 7.0
SWIGLU_ALPHA = 1.702

V7X_LANES = 128
V7X_VMEM_BYTES = 64 * 1024 * 1024
VMEM_CAP_BYTES = V7X_VMEM_BYTES - 6 * 1024 * 1024


def _params(semantics, vmem_bytes):
    limit = int(min(VMEM_CAP_BYTES, max(vmem_bytes * 5 // 4 + (4 << 20), 16 << 20)))
    return pltpu.CompilerParams(dimension_semantics=semantics, vmem_limit_bytes=limit)


def _tile(n, want):
    t = min(want, n)
    while n % t or t % 8:
        t -= 8 if t % 8 == 0 else t % 8
        assert t > 0, (n, want)
    return t


NT_DIMS = (((1,), (1,)), ((), ()))


def _dot(a, b):
    return jnp.dot(a, b, preferred_element_type=F32)


def _split3(v):
    hi = v.astype(BF16)
    r = v - hi.astype(F32)
    mid = r.astype(BF16)
    lo = (r - mid.astype(F32)).astype(BF16)
    return hi, mid, lo


def _dot_sel(m01, v):
    hi, mid, lo = _split3(v)
    return _dot(m01, hi) + _dot(m01, mid) + _dot(m01, lo)


def _sigmoid(x):
    return jax.nn.sigmoid(x)


def _rms(x, g):
    return x * lax.rsqrt(jnp.mean(x * x, axis=-1, keepdims=True) + EPS) * g


def _expand_rows(m_ref, i, tm, seq_len):
    if seq_len >= tm:
        return m_ref[pl.ds((i * tm) // seq_len, 1), :]
    nseq = tm // seq_len
    r = lax.broadcasted_iota(I32, (tm, nseq), 0)
    c = lax.broadcasted_iota(I32, (tm, nseq), 1) * seq_len
    sel = ((r >= c) & (r < c + seq_len)).astype(BF16)
    return _dot_sel(sel, m_ref[...])


def _table_spec(n_seq, d, tm, seq_len):
    if seq_len >= tm:
        return pl.BlockSpec((n_seq, d), lambda i, *_: (0, 0))
    return pl.BlockSpec((tm // seq_len, d), lambda i, *_: (i, 0))


def _ada_kernel(c_ref, w_ref, b_ref, o_ref):
    c = c_ref[...]
    s = (c * _sigmoid(c)).astype(BF16)
    o_ref[...] = _dot(s, w_ref[...].astype(BF16)) + b_ref[...]


def _ada(c_all, w_ada, b_ada):
    bc, d = c_all.shape
    n6 = w_ada.shape[1]
    tn = min(1024, n6)
    return pl.pallas_call(
        _ada_kernel,
        grid=(n6 // tn,),
        in_specs=[pl.BlockSpec((bc, d), lambda j: (0, 0)),
                  pl.BlockSpec((d, tn), lambda j: (0, j)),
                  pl.BlockSpec((1, tn), lambda j: (0, j))],
        out_specs=pl.BlockSpec((bc, tn), lambda j: (0, j)),
        out_shape=jax.ShapeDtypeStruct((bc, n6), F32),
        compiler_params=_params(("arbitrary",), 2 * d * tn * 4 + d * tn * 2 + 4 * bc * (d + tn) * 4),
        name="ada",
    )(c_all, w_ada, b_ada.reshape(1, n6))


IN_TN = 512


def _in_kernel(x_ref, sh_ref, sc_ref, g_ref, w_hbm, wlow_ref, wal_ref, bal_ref, u_ref, la_ref, w_vmem, sem,
               *, tm, seq_len):
    i = pl.program_id(0)
    load_w = pltpu.make_async_copy(w_hbm, w_vmem, sem)

    @pl.when(i == 0)
    def _():
        load_w.start()

    xn = _rms(x_ref[...], g_ref[...])
    h = xn * (1.0 + _expand_rows(sc_ref, i, tm, seq_len)) + _expand_rows(sh_ref, i, tm, seq_len)
    hb = h.astype(BF16)
    a_low = lax.dot_general(hb, wlow_ref[...].astype(BF16), NT_DIMS, preferred_element_type=F32)
    z = _dot(a_low.astype(BF16), wal_ref[...].astype(BF16)) + bal_ref[...]
    la_ref[...] = (jnp.minimum(z, 0.0) - jnp.log(1.0 + jnp.exp(-jnp.abs(z)))) * (1.0 / GLA_GATE_NORM)

    @pl.when(i == 0)
    def _():
        load_w.wait()

    for jn in range(u_ref.shape[1] // IN_TN):
        cols = slice(jn * IN_TN, (jn + 1) * IN_TN)
        u_ref[:, cols] = lax.dot_general(hb, w_vmem[cols, :], NT_DIMS, preferred_element_type=F32)


def _in_proj(x2, sh, sc, g, w_main_bf, w_low, w_alpha, b_alpha, seq_len):
    n, d = x2.shape
    n_main = w_main_bf.shape[0]
    n_seq = sh.shape[0]
    qk = w_alpha.shape[1]
    rank = w_low.shape[0]
    tm = _tile(n, 256)
    vmem = d * n_main * 2 + 2 * tm * d * 4 + 2 * tm * n_main * 4 + 2 * tm * qk * 4 + 4 * tm * d * 4
    fix = lambda i: (0, 0)
    return pl.pallas_call(
        functools.partial(_in_kernel, tm=tm, seq_len=seq_len),
        grid=(n // tm,),
        in_specs=[pl.BlockSpec((tm, d), lambda i: (i, 0)),
                  _table_spec(n_seq, d, tm, seq_len),
                  _table_spec(n_seq, d, tm, seq_len),
                  pl.BlockSpec((1, d), fix),
                  pl.BlockSpec(memory_space=pl.ANY),
                  pl.BlockSpec((rank, d), fix),
                  pl.BlockSpec((rank, qk), fix),
                  pl.BlockSpec((1, qk), fix)],
        out_specs=[pl.BlockSpec((tm, n_main), lambda i: (i, 0)),
                   pl.BlockSpec((tm, qk), lambda i: (i, 0))],
        out_shape=[jax.ShapeDtypeStruct((n, n_main), F32), jax.ShapeDtypeStruct((n, qk), F32)],
        scratch_shapes=[pltpu.VMEM((n_main, d), BF16), pltpu.SemaphoreType.DMA],
        compiler_params=_params(("arbitrary",), vmem),
        name="in_proj",
    )(x2, sh, sc, g.reshape(1, d), w_main_bf, w_low, w_alpha, b_alpha.reshape(1, qk))


CONV_HALO = 32


def _ln_silu(y, g, b):
    mu = jnp.mean(y, axis=-1, keepdims=True)
    yc = y - mu
    var = jnp.mean(yc * yc, axis=-1, keepdims=True)
    yn = yc * lax.rsqrt(var + EPS) * g + b
    return yn * _sigmoid(yn)


def _conv_seq_kernel(ua_ref, ub_ref, w_ref, b_ref, lg_ref, lb_ref, y_ref, nb_ref, buf, shifted, *, tt, kw):
    t = pl.program_id(1)
    cc = buf.shape[1]

    @pl.when(t == 0)
    def _():
        buf[0:CONV_HALO, :] = jnp.zeros((CONV_HALO, cc), F32)

    buf[CONV_HALO:CONV_HALO + tt, :] = ua_ref[...] * _sigmoid(ub_ref[...])
    span = shifted.shape[1]
    for r in range(1, 8):
        shifted[r - 1] = buf[r:r + span, :]
    first = CONV_HALO - (kw - 1)

    acc = jnp.broadcast_to(b_ref[...], (tt, cc))
    for j in range(kw):
        q, r = divmod(first + j, 8)
        win = buf[8 * q:8 * q + tt, :] if r == 0 else shifted[r - 1, 8 * q:8 * q + tt, :]
        acc = acc + w_ref[j:j + 1, :] * win
    y_ref[...] = _ln_silu(acc, lg_ref[...], lb_ref[...]).astype(y_ref.dtype)

    @pl.when(t == pl.num_programs(1) - 1)
    def _():
        nb_ref[...] = buf[CONV_HALO + tt - (kw - 1):CONV_HALO + tt, :]

    buf[0:CONV_HALO, :] = buf[tt:tt + CONV_HALO, :]


def _conv_seq(u2, n_seq, seq_len, w_dw, b_dw, ln_g, ln_b):
    kw, cc = w_dw.shape
    assert kw - 1 <= CONV_HALO
    tt = _tile(seq_len, 256)
    assert tt % 8 == 0 and tt >= CONV_HALO
    nt = seq_len // tt
    span = tt + CONV_HALO - 8
    row = lambda v: v.reshape(1, cc)
    vmem = 4 * tt * cc * 4 + 2 * tt * cc * 2 + (CONV_HALO + tt + 7 * span) * cc * 4 + 6 * tt * cc * 4
    return pl.pallas_call(
        functools.partial(_conv_seq_kernel, tt=tt, kw=kw),
        grid=(n_seq, nt),
        in_specs=[pl.BlockSpec((tt, cc), lambda b, t: (b * nt + t, 0)),
                  pl.BlockSpec((tt, cc), lambda b, t: (b * nt + t, 1)),
                  pl.BlockSpec((kw, cc), lambda b, t: (0, 0)),
                  pl.BlockSpec((1, cc), lambda b, t: (0, 0)),
                  pl.BlockSpec((1, cc), lambda b, t: (0, 0)),
                  pl.BlockSpec((1, cc), lambda b, t: (0, 0))],
        out_specs=[pl.BlockSpec((tt, cc), lambda b, t: (b * nt + t, 0)),
                   pl.BlockSpec((None, None, kw - 1, cc), lambda b, t: (0, b, 0, 0))],
        out_shape=[jax.ShapeDtypeStruct((n_seq * seq_len, cc), BF16),
                   jax.ShapeDtypeStruct((1, n_seq, kw - 1, cc), F32)],
        scratch_shapes=[pltpu.VMEM((CONV_HALO + tt, cc), F32), pltpu.VMEM((7, span, cc), F32)],
        compiler_params=_params(("arbitrary", "arbitrary"), vmem),
        name="conv_seq",
    )(u2, u2, w_dw, row(b_dw), row(ln_g), row(ln_b))


def _conv_step_kernel(ua_ref, ub_ref, st_ref, w_ref, b_ref, lg_ref, lb_ref, y_ref, nb_ref, gbuf, ybuf,
                      *, kw, t_len):
    hist = kw - 1
    n_seq, cc = st_ref.shape[1], st_ref.shape[2]
    n_lt = gbuf.shape[0]
    lanes = gbuf.shape[2]
    glu = ua_ref[...] * _sigmoid(ub_ref[...])
    for lt in range(n_lt):
        gbuf[lt] = glu[:, lt * lanes:(lt + 1) * lanes]

    def slab(r):
        if r < hist:
            return st_ref[r]
        step = pl.ds(r - hist, n_seq, stride=t_len)
        return jnp.concatenate([gbuf[lt, step, :] for lt in range(n_lt)], axis=1)

    for t in range(t_len):
        acc = jnp.broadcast_to(b_ref[...], (n_seq, cc))
        for j in range(kw):
            acc = acc + w_ref[j:j + 1, :] * slab(t + j)
        y = _ln_silu(acc, lg_ref[...], lb_ref[...])
        for lt in range(n_lt):
            ybuf[lt, pl.ds(t, n_seq, stride=t_len), :] = y[:, lt * lanes:(lt + 1) * lanes]
    y_ref[...] = jnp.concatenate([ybuf[lt] for lt in range(n_lt)], axis=1)
    for r in range(hist):
        nb_ref[r] = slab(r + t_len)


def _conv_step(u2, state_t, t_len, w_dw, b_dw, ln_g, ln_b):
    kw, cc = w_dw.shape
    n_seq = state_t.shape[2]
    bs = _tile(n_seq, 16)
    tb = bs * t_len
    row = lambda v: v.reshape(1, cc)
    vmem = 4 * tb * cc * 4 + 2 * tb * cc * 4 + 4 * (kw - 1) * bs * cc * 4 + tb * cc * 4 + 4 * tb * cc * 4
    hist_spec = pl.BlockSpec((None, kw - 1, bs, cc), lambda i: (0, 0, i, 0))
    return pl.pallas_call(
        functools.partial(_conv_step_kernel, kw=kw, t_len=t_len),
        grid=(n_seq // bs,),
        in_specs=[pl.BlockSpec((tb, cc), lambda i: (i, 0)),
                  pl.BlockSpec((tb, cc), lambda i: (i, 1)),
                  hist_spec,
                  pl.BlockSpec((kw, cc), lambda i: (0, 0)),
                  pl.BlockSpec((1, cc), lambda i: (0, 0)),
                  pl.BlockSpec((1, cc), lambda i: (0, 0)),
                  pl.BlockSpec((1, cc), lambda i: (0, 0))],
        out_specs=[pl.BlockSpec((tb, cc), lambda i: (i, 0)), hist_spec],
        out_shape=[jax.ShapeDtypeStruct((n_seq * t_len, cc), F32),
                   jax.ShapeDtypeStruct(state_t.shape, F32)],
        scratch_shapes=[pltpu.VMEM((cc // V7X_LANES, tb, V7X_LANES), F32),
                        pltpu.VMEM((cc // V7X_LANES, tb, V7X_LANES), F32)],
        compiler_params=_params(("arbitrary",), vmem),
        name="conv_step",
    )(u2, u2, state_t, w_dw, row(b_dw), row(ln_g), row(ln_b))


def _gla_consts(c):
    n_lev = int(np.log2(c))
    i = np.arange(c)
    mall = [(i[None, :] <= i[:, None])]
    masks = []
    for lev in range(n_lev):
        s = 1 << lev
        m = (i // (2 * s)) * (2 * s) + s - 1
        mall.append(i[None, :] <= m[:, None])
        same = (i[:, None] // (2 * s)) == (i[None, :] // (2 * s))
        upper = ((i[:, None] // s) % 2) == 1
        lower = ((i[None, :] // s) % 2) == 0
        masks.append(same & upper & lower)
    mall = jnp.asarray(np.concatenate(mall, axis=0).astype(np.float32), dtype=BF16)
    masks = jnp.asarray(np.stack(masks).astype(np.float32))
    return mall, masks, n_lev


def _gla_intra(units, mask_ref, c, n_lev):
    nt_dims = (((1,), (1,)), ((), ()))
    tn_dims = (((0,), (0,)), ((), ()))
    staged = []
    for q, k, v, ball in units:
        bc = ball[0:c]
        b_last = bc[c - 1:c, :]
        levels = []
        for lev in range(n_lev):
            bm = ball[(1 + lev) * c:(2 + lev) * c]
            decay = jnp.exp(-jnp.abs(bc - bm))
            levels.append(((q * decay).astype(BF16), (k * decay).astype(BF16)))
        staged.append(dict(qe=(q * jnp.exp(bc)).astype(BF16), kd=(k * jnp.exp(b_last - bc)).astype(BF16),
                           vb=v.astype(BF16), diag=jnp.sum(q * k, axis=-1, keepdims=True) * v,
                           b_last=b_last, levels=levels))
    atts = []
    for u in staged:
        att = None
        for lev, (ql, kl) in enumerate(u["levels"]):
            a = lax.dot_general(ql, kl, nt_dims, preferred_element_type=F32)
            a = jnp.where(mask_ref[lev] > 0.5, a, 0.0)
            att = a if att is None else att + a
        atts.append(att)
    out = []
    for u, att in zip(staged, atts):
        dk = u["kd"].shape[1]
        dv = u["vb"].shape[1]
        o_intra = _dot(att.astype(BF16), u["vb"]) + u["diag"]
        upd = lax.dot_general(u["kd"], u["vb"], tn_dims, preferred_element_type=F32)
        dec = jnp.exp(jnp.transpose(jnp.broadcast_to(u["b_last"], (dk, dk))))
        out.append((u["qe"], o_intra, upd, jnp.tile(dec, (1, dv // dk))))
    return out


def _gla_finish(o, g, gn):
    return _rms(o, gn) * (g * _sigmoid(g))


GLA_SEQ_CHUNKS = 4
GLA_STEP_SEQS = 4


def _gla_seq_kernel(q_ref, k_ref, v_ref, g_ref, la_ref, gn_ref, mall_ref, mask_ref, y_ref, sf_ref, s_scr,
                    *, c, n_lev, heads):
    t = pl.program_id(1)
    dk = q_ref.shape[1] // heads
    dv = v_ref.shape[1] // heads
    n_ch = q_ref.shape[0] // c

    @pl.when(t == 0)
    def _():
        s_scr[...] = jnp.zeros(s_scr.shape, F32)

    mall = mall_ref[...]
    units = []
    for ci in range(n_ch):
        rows = slice(ci * c, (ci + 1) * c)
        ball = _dot_sel(mall, la_ref[rows, :])
        for h in range(heads):
            ks = slice(h * dk, (h + 1) * dk)
            units.append((q_ref[rows, ks] * (dk ** -0.5), k_ref[rows, ks], v_ref[rows, h * dv:(h + 1) * dv],
                          ball[:, ks]))
    parts = _gla_intra(units, mask_ref, c, n_lev)
    state = [s_scr[h] for h in range(heads)]
    for ci in range(n_ch):
        rows = slice(ci * c, (ci + 1) * c)
        for h in range(heads):
            vs = slice(h * dv, (h + 1) * dv)
            qe, o_intra, upd, dec = parts[ci * heads + h]
            o = o_intra + _dot(qe, state[h].astype(BF16))
            state[h] = state[h] * dec + upd
            y_ref[rows, vs] = _gla_finish(o, g_ref[rows, vs], gn_ref[...]).astype(y_ref.dtype)
    for h in range(heads):
        s_scr[h] = state[h]

    @pl.when(t == pl.num_programs(1) - 1)
    def _():
        sf_ref[...] = s_scr[...]


def _gla_seq(u2, la, n_seq, seq_len, gn_g, cc, qk):
    heads = GLA_HEADS
    dk = qk // heads
    dv = cc // heads
    c = int(np.gcd(seq_len, GLA_CHUNK))
    n_ch = GLA_SEQ_CHUNKS if (seq_len // c) % GLA_SEQ_CHUNKS == 0 else 1
    tb = n_ch * c
    nt = seq_len // tb
    mall, masks, n_lev = _gla_consts(c)
    qb = 2 * cc // qk
    vb = (2 * cc + 2 * qk) // cc
    vmem = 2 * tb * (3 * qk + 3 * cc) * 4 + 3 * heads * dk * dv * 4 + 96 * n_ch * heads * c * max(dv, 128) * 4
    return pl.pallas_call(
        functools.partial(_gla_seq_kernel, c=c, n_lev=n_lev, heads=heads),
        grid=(n_seq, nt),
        in_specs=[pl.BlockSpec((tb, qk), lambda b, t: (b * nt + t, qb)),
                  pl.BlockSpec((tb, qk), lambda b, t: (b * nt + t, qb + 1)),
                  pl.BlockSpec((tb, cc), lambda b, t: (b * nt + t, vb)),
                  pl.BlockSpec((tb, cc), lambda b, t: (b * nt + t, vb + 1)),
                  pl.BlockSpec((tb, qk), lambda b, t: (b * nt + t, 0)),
                  pl.BlockSpec((1, dv), lambda b, t: (0, 0)),
                  pl.BlockSpec(mall.shape, lambda b, t: (0, 0)),
                  pl.BlockSpec(masks.shape, lambda b, t: (0, 0, 0))],
        out_specs=[pl.BlockSpec((tb, cc), lambda b, t: (b * nt + t, 0)),
                   pl.BlockSpec((None, None, heads, dk, dv), lambda b, t: (0, b, 0, 0, 0))],
        out_shape=[jax.ShapeDtypeStruct((n_seq * seq_len, cc), BF16),
                   jax.ShapeDtypeStruct((1, n_seq, heads, dk, dv), F32)],
        scratch_shapes=[pltpu.VMEM((heads, dk, dv), F32)],
        compiler_params=_params(("arbitrary", "arbitrary"), vmem),
        name="gla_seq",
    )(u2, u2, u2, u2, la, gn_g.reshape(1, dv), mall, masks)


def _gla_step_kernel(q_ref, k_ref, v_ref, g_ref, la_ref, st_ref, gn_ref, mall_ref, mask_ref, y_ref, sn_ref,
                     *, c, n_lev, heads):
    dk = q_ref.shape[2] // heads
    dv = v_ref.shape[2] // heads
    mall = mall_ref[...]

    n_un = GLA_STEP_SEQS if q_ref.shape[0] % GLA_STEP_SEQS == 0 else 1

    def body(t, carry):
        units = []
        for ss in range(n_un):
            s = t * n_un + ss
            ball = _dot_sel(mall, la_ref[s])
            for h in range(heads):
                ks = slice(h * dk, (h + 1) * dk)
                units.append((q_ref[s, :, ks] * (dk ** -0.5), k_ref[s, :, ks], v_ref[s, :, h * dv:(h + 1) * dv],
                              ball[:, ks]))
        parts = _gla_intra(units, mask_ref, c, n_lev)
        for ss in range(n_un):
            s = t * n_un + ss
            for h in range(heads):
                vs = slice(h * dv, (h + 1) * dv)
                qe, o_intra, upd, dec = parts[ss * heads + h]
                s0 = st_ref[s, h]
                o = o_intra + _dot(qe, s0.astype(BF16))
                sn_ref[s, h] = s0 * dec + upd
                y_ref[s, :, vs] = _gla_finish(o, g_ref[s, :, vs], gn_ref[...])
        return carry

    lax.fori_loop(0, q_ref.shape[0] // n_un, body, 0)


def _gla_step(u3, la3, state, gn_g, cc, qk):
    n_seq, t_len, _ = u3.shape
    heads = GLA_HEADS
    dk = qk // heads
    dv = cc // heads
    assert t_len <= GLA_CHUNK and GLA_CHUNK % t_len == 0, "single-chunk path"
    mall, masks, n_lev = _gla_consts(t_len)
    bs = min(8, n_seq)
    qb = 2 * cc // qk
    vb = (2 * cc + 2 * qk) // cc
    vmem = 2 * bs * t_len * (3 * qk + 3 * cc) * 4 + 4 * bs * heads * dk * dv * 4 + 64 * 8 * max(dv, 128) * 4
    return pl.pallas_call(
        functools.partial(_gla_step_kernel, c=t_len, n_lev=n_lev, heads=heads),
        grid=(n_seq // bs,),
        in_specs=[pl.BlockSpec((bs, t_len, qk), lambda i: (i, 0, qb)),
                  pl.BlockSpec((bs, t_len, qk), lambda i: (i, 0, qb + 1)),
                  pl.BlockSpec((bs, t_len, cc), lambda i: (i, 0, vb)),
                  pl.BlockSpec((bs, t_len, cc), lambda i: (i, 0, vb + 1)),
                  pl.BlockSpec((bs, t_len, qk), lambda i: (i, 0, 0)),
                  pl.BlockSpec((None, bs, heads, dk, dv), lambda i: (0, i, 0, 0, 0)),
                  pl.BlockSpec((1, dv), lambda i: (0, 0)),
                  pl.BlockSpec(mall.shape, lambda i: (0, 0)),
                  pl.BlockSpec(masks.shape, lambda i: (0, 0, 0))],
        out_specs=[pl.BlockSpec((bs, t_len, cc), lambda i: (i, 0, 0)),
                   pl.BlockSpec((None, bs, heads, dk, dv), lambda i: (0, i, 0, 0, 0))],
        out_shape=[jax.ShapeDtypeStruct((n_seq, t_len, cc), F32),
                   jax.ShapeDtypeStruct((1, n_seq, heads, dk, dv), F32)],
        compiler_params=_params(("arbitrary",), vmem),
        name="gla_step",
    )(u3, u3, u3, u3, la3, state, gn_g.reshape(1, dv), mall, masks)


def _out_kernel(x_ref, yc_ref, yg_ref, wo_ref, g1_ref, sh_ref, sc_ref, n2_ref, wr_ref, br_ref,
                x1_ref, h2_ref, ti_ref, tw_ref, *, tm, seq_len):
    i = pl.program_id(0)
    cc = yc_ref.shape[1]
    n_exp = wr_ref.shape[1]
    mix = _dot(yc_ref[...].astype(BF16), wo_ref[0:cc, :]) + _dot(yg_ref[...].astype(BF16), wo_ref[cc:, :])
    x1 = x_ref[...] + _expand_rows(g1_ref, i, tm, seq_len) * mix
    x1_ref[...] = x1
    h2 = _rms(x1, n2_ref[...]) * (1.0 + _expand_rows(sc_ref, i, tm, seq_len)) + _expand_rows(sh_ref, i, tm, seq_len)
    h2_ref[...] = h2

    logits = _dot(h2.astype(BF16), wr_ref[...].astype(BF16)) + br_ref[...]
    lane = lax.broadcasted_iota(I32, logits.shape, 1)
    vals = []
    for r in range(TOP_K):
        m = jnp.max(logits, axis=-1, keepdims=True)
        idx = jnp.min(jnp.where(logits == m, lane, n_exp), axis=-1, keepdims=True)
        vals.append(m)
        ti_ref[:, r:r + 1] = idx
        logits = jnp.where(lane == idx, -jnp.inf, logits)
    ex = [jnp.exp(v - vals[0]) for v in vals]
    den = ex[0] + ex[1] + ex[2] + ex[3]
    for r in range(TOP_K):
        tw_ref[:, r:r + 1] = ex[r] / den


def _out_proj(x2, yc, yg, wo_bf, g1, sh2, sc2, n2, w_router, b_router, seq_len):
    n, d = x2.shape
    cc = yc.shape[1]
    n_seq = g1.shape[0]
    n_exp = w_router.shape[1]
    tm = _tile(n, 256)
    vmem = (6 * tm * d * 4 + 4 * tm * cc * 4 + 2 * 2 * cc * d * 2 + 8 * tm * d * 4)
    tok = lambda i: (i, 0)
    fix = lambda i: (0, 0)
    return pl.pallas_call(
        functools.partial(_out_kernel, tm=tm, seq_len=seq_len),
        grid=(n // tm,),
        in_specs=[pl.BlockSpec((tm, d), tok),
                  pl.BlockSpec((tm, cc), tok),
                  pl.BlockSpec((tm, cc), tok),
                  pl.BlockSpec((2 * cc, d), fix),
                  _table_spec(n_seq, d, tm, seq_len),
                  _table_spec(n_seq, d, tm, seq_len),
                  _table_spec(n_seq, d, tm, seq_len),
                  pl.BlockSpec((1, d), fix),
                  pl.BlockSpec((d, n_exp), fix),
                  pl.BlockSpec((1, n_exp), fix)],
        out_specs=[pl.BlockSpec((tm, d), tok),
                   pl.BlockSpec((tm, d), tok),
                   pl.BlockSpec((tm, TOP_K), tok),
                   pl.BlockSpec((tm, TOP_K), tok)],
        out_shape=[jax.ShapeDtypeStruct((n, d), F32),
                   jax.ShapeDtypeStruct((n, d), F32),
                   jax.ShapeDtypeStruct((n, TOP_K), I32),
                   jax.ShapeDtypeStruct((n, TOP_K), F32)],
        compiler_params=_params(("arbitrary",), vmem),
        name="out_proj",
    )(x2, yc, yg, wo_bf, g1, sh2, sc2, n2.reshape(1, d), w_router, b_router.reshape(1, n_exp))


def _rank_kernel(ti_ref, slot_ref, cnt_ref, off_ref, carry, offs, *, tr, n_exp, group):
    p = pl.program_id(0)
    i = pl.program_id(1)
    lane = lax.broadcasted_iota(I32, (tr, n_exp), 1)
    ti = ti_ref[...]
    hot = jnp.zeros((tr, n_exp), F32)
    for k in range(TOP_K):
        hot = hot + (lane == ti[:, k:k + 1]).astype(F32)

    @pl.when((p == 0) & (i == 0))
    def _():
        carry[...] = jnp.zeros(carry.shape, F32)

    @pl.when((p == 1) & (i == 0))
    def _():
        cnt = carry[0:1, :]
        padded = jnp.ceil(cnt * (1.0 / group)) * group
        r = lax.broadcasted_iota(I32, (n_exp, n_exp), 0)
        c = lax.broadcasted_iota(I32, (n_exp, n_exp), 1)
        before = (r < c).astype(BF16)
        hi, mid, lo = _split3(jnp.broadcast_to(padded, (8, n_exp)))
        off = _dot(hi, before) + _dot(mid, before) + _dot(lo, before)
        offs[...] = off
        cnt_ref[...] = cnt.astype(I32)
        off_ref[...] = off[0:1, :].astype(I32)
        carry[...] = jnp.zeros(carry.shape, F32)

    @pl.when(p == 1)
    def _():
        r = lax.broadcasted_iota(I32, (tr, tr), 0)
        c = lax.broadcasted_iota(I32, (tr, tr), 1)
        earlier = (c < r).astype(BF16)
        tot = _dot(earlier, hot.astype(BF16)) + carry[0:1, :] + offs[0:1, :]
        for k in range(TOP_K):
            sel = jnp.where(lane == ti[:, k:k + 1], tot, 0.0)
            slot_ref[:, k:k + 1] = jnp.sum(sel, axis=-1, keepdims=True).astype(I32)

    carry[...] = carry[...] + jnp.sum(hot, axis=0, keepdims=True)


def _rank(top_i, n_exp, group):
    n = top_i.shape[0]
    tr = _tile(n, 512)
    return pl.pallas_call(
        functools.partial(_rank_kernel, tr=tr, n_exp=n_exp, group=group),
        grid=(2, n // tr),
        in_specs=[pl.BlockSpec((tr, TOP_K), lambda p, i: (i, 0))],
        out_specs=[pl.BlockSpec((tr, TOP_K), lambda p, i: (i * p, 0)),
                   pl.BlockSpec((1, n_exp), lambda p, i: (0, 0)),
                   pl.BlockSpec((1, n_exp), lambda p, i: (0, 0))],
        out_shape=[jax.ShapeDtypeStruct((n, TOP_K), I32),
                   jax.ShapeDtypeStruct((1, n_exp), I32),
                   jax.ShapeDtypeStruct((1, n_exp), I32)],
        scratch_shapes=[pltpu.VMEM((8, n_exp), F32), pltpu.VMEM((8, n_exp), F32)],
        compiler_params=_params(("arbitrary", "arbitrary"), 8 * tr * tr * 4),
        name="rank",
    )(top_i)


def _scatter_rows(slot_ref, h_ref, xs_ref, sem, td):
    def issue(r, carry):
        for k in range(TOP_K):
            s = slot_ref[0, r * TOP_K + k]
            pltpu.make_async_copy(h_ref.at[pl.ds(r, 1), :], xs_ref.at[s], sem).start()
        return carry

    lax.fori_loop(0, td, issue, 0)
    for k in range(TOP_K):
        pltpu.make_async_copy(h_ref, xs_ref.at[pl.ds(0, td), 0, :], sem).wait()


def _dispatch_kernel(cnt, off, slot_ref, ha_ref, hb_ref, xs_ref, zbuf, sem, zsem, *, td, group, n_a):
    n_exp = cnt.shape[0]
    n_rows = xs_ref.shape[0]
    i = pl.program_id(0)

    def padded(e):
        return (cnt[e] + group - 1) // group * group

    def z_row(row):
        return pltpu.make_async_copy(zbuf.at[pl.ds(0, 1), :], xs_ref.at[row], zsem)

    def z_block(row):
        return pltpu.make_async_copy(zbuf, xs_ref.at[pl.ds(pl.multiple_of(row, group), group), 0, :], zsem)

    def for_pad_rows(fn):
        def expert(e, carry):
            lax.fori_loop(off[e] + cnt[e], off[e] + padded(e), lambda r, c: (fn(r), c)[1], 0)
            return carry
        lax.fori_loop(0, n_exp, expert, 0)

    def for_tail_blocks(fn):
        first = (off[n_exp - 1] + padded(n_exp - 1)) // group
        lax.fori_loop(first, n_rows // group, lambda b, c: (fn(b * group), c)[1], 0)

    @pl.when(i == 0)
    def _():
        zbuf[...] = jnp.zeros(zbuf.shape, zbuf.dtype)
        for_pad_rows(lambda r: z_row(r).start())
        for_tail_blocks(lambda r: z_block(r).start())

    @pl.when(i < n_a)
    def _():
        _scatter_rows(slot_ref, ha_ref, xs_ref, sem, td)

    @pl.when(i >= n_a)
    def _():
        _scatter_rows(slot_ref, hb_ref, xs_ref, sem, td)

    @pl.when(i == pl.num_programs(0) - 1)
    def _():
        for_pad_rows(lambda r: z_row(0).wait())
        for_tail_blocks(lambda r: z_block(0).wait())


def _dispatch(slots, h_a, h_b, cnt, off, n_rows, group):
    d = h_a.shape[1]
    td = _tile(int(np.gcd(h_a.shape[0], h_b.shape[0])), 256)
    n_a, n_b = h_a.shape[0] // td, h_b.shape[0] // td
    grid_spec = pltpu.PrefetchScalarGridSpec(
        num_scalar_prefetch=2,
        grid=(n_a + n_b,),
        in_specs=[pl.BlockSpec((None, 1, td * TOP_K), lambda i, *_: (i, 0, 0), memory_space=pltpu.SMEM),
                  pl.BlockSpec((td, d), lambda i, *_: (jnp.minimum(i, n_a - 1), 0)),
                  pl.BlockSpec((td, d), lambda i, *_: (jnp.maximum(i - n_a, 0), 0))],
        out_specs=pl.BlockSpec(memory_space=pl.ANY),
        scratch_shapes=[pltpu.VMEM((group, d), h_a.dtype), pltpu.SemaphoreType.DMA, pltpu.SemaphoreType.DMA])
    return pl.pallas_call(
        functools.partial(_dispatch_kernel, td=td, group=group, n_a=n_a),
        grid_spec=grid_spec,
        out_shape=jax.ShapeDtypeStruct((n_rows, 1, d), h_a.dtype),
        compiler_params=_params(("arbitrary",), 4 * td * d * 4 + group * d * 4),
        name="dispatch",
    )(cnt, off, slots.reshape(n_a + n_b, 1, td * TOP_K), h_a, h_b)


MOE_CHUNK = 1536
MOE_SUB = 128
MOE_MM = 2
MOE_TF = 256
MOE_TN = 512


def _for_row_blocks(n_units, body):
    n = n_units // MOE_MM

    def block(b):
        body(b * MOE_MM, MOE_MM)

    def trip(t, carry):
        for u in range(4):
            block(t * 4 + u)
        return carry

    lax.fori_loop(0, n // 4, trip, 0)

    @pl.when((n // 2) % 2 == 1)
    def _():
        block((n // 4) * 4)
        block((n // 4) * 4 + 1)

    @pl.when(n % 2 == 1)
    def _():
        block(n - 1)

    @pl.when(n_units % MOE_MM == 1)
    def _():
        body(n_units - 1, 1)


def _moe_kernel(che, chrow, chn, tail, xs_ref, wg_ref, wu_ref, bg_ref, bu_ref, wd_ref, bd_ref, yb_ref,
                xbuf, xb, act, wgb, wub, wdb, obuf, semx, semo, *, n_f, n_n, n_chunks):
    c = pl.program_id(0)
    j = pl.program_id(1)
    nsb = chn[c]
    sub = MOE_SUB

    def x_copy(cc, i):
        row = pl.multiple_of(chrow[cc] + i * sub, sub)
        return pltpu.make_async_copy(xs_ref.at[pl.ds(row, sub), 0, :],
                                     xbuf.at[pl.ds(pl.multiple_of(i * sub, sub), sub), :], semx)

    def x_fetch(cc):
        lax.fori_loop(0, chn[cc], lambda i, carry: (x_copy(cc, i).start(), carry)[1], 0)

    def o_copy(slot, i, jj):
        row = pl.multiple_of(chrow[c] + i * sub, sub)
        col = pl.multiple_of(jj * MOE_TN, MOE_TN)
        return pltpu.make_async_copy(obuf.at[slot, pl.ds(pl.multiple_of(i * sub, sub), sub), :],
                                     yb_ref.at[pl.ds(row, sub), 0, pl.ds(col, MOE_TN)], semo.at[slot])

    @pl.when((c == 0) & (j == 0))
    def _():
        x_fetch(0)

    @pl.when(j == 0)
    def _():
        lax.fori_loop(0, nsb, lambda i, carry: (x_copy(c, 0).wait(), carry)[1], 0)

        def cast(i, carry):
            rows = pl.ds(pl.multiple_of(i * sub, sub), sub)
            xb[rows, :] = xbuf[rows, :].astype(BF16)
            return carry

        lax.fori_loop(0, nsb, cast, 0)

    @pl.when((j == 1) & (c + 1 < n_chunks))
    def _():
        x_fetch(jnp.minimum(c + 1, n_chunks - 1))

    @pl.when((j < n_f) & (nsb > 0))
    def _():
        wgb[...] = wg_ref[...].astype(BF16)
        wub[...] = wu_ref[...].astype(BF16)
        tile_row = pl.ds(che[c] * n_f + j, 1)
        bg = bg_ref[tile_row, :]
        bu = bu_ref[tile_row, :]

        def row_block(u, n):
            rows = pl.ds(pl.multiple_of(u * sub, sub), n * sub)
            x = xb[rows, :]
            gate = jnp.minimum(_dot(x, wgb[...]) + bg, SWIGLU_LIMIT)
            up = jnp.clip(_dot(x, wub[...]) + bu, -SWIGLU_LIMIT, SWIGLU_LIMIT)
            glu = gate * _sigmoid(SWIGLU_ALPHA * gate)
            act[j, rows, :] = ((up + 1.0) * glu).astype(BF16)

        _for_row_blocks(nsb, row_block)

    @pl.when((j >= n_f) & (nsb > 0))
    def _():
        jj = j - n_f
        slot = jj % 2
        wdb[...] = wd_ref[...].astype(BF16)
        bd = bd_ref[pl.ds(che[c] * n_n + jj, 1), :]

        def retire(s):
            def one(i, carry):
                o_copy(s, 0, 0).wait()
                return carry
            lax.fori_loop(0, nsb, one, 0)

        @pl.when(jj >= 2)
        def _():
            retire(slot)

        def row_block(u, n):
            rows = pl.ds(pl.multiple_of(u * sub, sub), n * sub)
            a = jnp.concatenate([act[f, rows, :] for f in range(n_f)], axis=1)
            obuf[slot, rows, :] = _dot(a, wdb[...]) + bd
            for k in range(n):
                o_copy(slot, u + k, jj).start()

        _for_row_blocks(nsb, row_block)

        @pl.when(jj == n_n - 1)
        def _():
            retire(slot)
            if n_n >= 2:
                retire(1 - slot)

    @pl.when((c == n_chunks - 1) & (j == n_f + n_n - 1))
    def _():
        n_tail = (yb_ref.shape[0] - tail[0]) // sub
        obuf[0, 0:sub, :] = jnp.zeros((sub, MOE_TN), F32)

        def z_copy(i, js):
            row = pl.multiple_of(tail[0] + i * sub, sub)
            return pltpu.make_async_copy(obuf.at[0, pl.ds(0, sub), :],
                                         yb_ref.at[pl.ds(row, sub), 0, pl.ds(js * MOE_TN, MOE_TN)], semo.at[0])

        def fill(i, carry):
            for js in range(n_n):
                z_copy(i, js).start()
            return carry

        def drain(i, carry):
            for js in range(n_n):
                z_copy(0, js).wait()
            return carry

        lax.fori_loop(0, n_tail, fill, 0)
        lax.fori_loop(0, n_tail, drain, 0)


def _moe(xs, yb_rows, ch_e, ch_row, ch_n, tail, w_gate, b_gate, w_up, b_up, w_down, b_down):
    n_exp, d, f = w_gate.shape
    n_f = f // MOE_TF
    n_n = d // MOE_TN
    n_chunks = ch_e.shape[0]

    def w1(c, j, che, chrow, chn, tail):
        return (che[c], 0, jnp.where(chn[c] > 0, jnp.minimum(j, n_f - 1), n_f - 1))

    def w2(c, j, che, chrow, chn, tail):
        return (che[c], 0, jnp.where(chn[c] > 0, jnp.maximum(j - n_f, 0), n_n - 1))

    def fixed(c, j, che, chrow, chn, tail):
        return (0, 0)

    vmem = (MOE_CHUNK * d * 4 + MOE_CHUNK * d * 2 + MOE_CHUNK * f * 2
            + 2 * 2 * d * MOE_TF * 4 + 2 * f * MOE_TN * 4 + 2 * d * MOE_TF * 2 + f * MOE_TN * 2
            + 2 * MOE_CHUNK * MOE_TN * 4 + 8 * MOE_MM * MOE_SUB * max(d, f) * 4)
    grid_spec = pltpu.PrefetchScalarGridSpec(
        num_scalar_prefetch=4,
        grid=(n_chunks, n_f + n_n),
        in_specs=[pl.BlockSpec(memory_space=pl.ANY),
                  pl.BlockSpec((None, d, MOE_TF), w1),
                  pl.BlockSpec((None, d, MOE_TF), w1),
                  pl.BlockSpec((n_exp * n_f, MOE_TF), fixed),
                  pl.BlockSpec((n_exp * n_f, MOE_TF), fixed),
                  pl.BlockSpec((None, f, MOE_TN), w2),
                  pl.BlockSpec((n_exp * n_n, MOE_TN), fixed)],
        out_specs=pl.BlockSpec(memory_space=pl.ANY),
        scratch_shapes=[pltpu.VMEM((MOE_CHUNK, d), F32),
                        pltpu.VMEM((MOE_CHUNK, d), BF16),
                        pltpu.VMEM((n_f, MOE_CHUNK, MOE_TF), BF16),
                        pltpu.VMEM((d, MOE_TF), BF16),
                        pltpu.VMEM((d, MOE_TF), BF16),
                        pltpu.VMEM((f, MOE_TN), BF16),
                        pltpu.VMEM((2, MOE_CHUNK, MOE_TN), F32),
                        pltpu.SemaphoreType.DMA,
                        pltpu.SemaphoreType.DMA((2,))])
    return pl.pallas_call(
        functools.partial(_moe_kernel, n_f=n_f, n_n=n_n, n_chunks=n_chunks),
        grid_spec=grid_spec,
        out_shape=jax.ShapeDtypeStruct((yb_rows, 1, d), F32),
        compiler_params=_params(("arbitrary", "arbitrary"), vmem),
        name="moe",
    )(ch_e, ch_row, ch_n, tail, xs, w_gate, w_up, b_gate.reshape(n_exp * n_f, MOE_TF),
      b_up.reshape(n_exp * n_f, MOE_TF), w_down, b_down.reshape(n_exp * n_n, MOE_TN))


def _chunk_table(cnt, off, n_chunks):
    n_exp = cnt.shape[0]
    per = MOE_CHUNK // MOE_SUB
    nsb = (cnt + MOE_SUB - 1) // MOE_SUB
    nch = (nsb + per - 1) // per
    end = jnp.cumsum(nch)
    start = end - nch
    cidx = jnp.arange(n_chunks, dtype=I32)
    owner = lambda ci: jnp.minimum(jnp.sum((end[None, :] <= ci[:, None]).astype(I32), axis=1), n_exp - 1)
    e = owner(cidx)
    used = cidx < end[-1]
    last_e = owner(end[-1:] - 1)[0]
    local = cidx - start[e]
    ch_e = jnp.where(used, e, last_e)
    ch_row = jnp.where(used, off[e] + local * MOE_CHUNK, 0).astype(I32)
    ch_n = jnp.where(used, jnp.minimum(per, nsb[e] - local * per), 0).astype(I32)
    tail = (off[-1] + nsb[-1] * MOE_SUB).astype(I32).reshape(1)
    return ch_e, ch_row, ch_n, tail


def _combine_kernel(slot_ref, next_slot_ref, tw_ref, x1_ref, g2_ref, gf_ref, yb_ref, y_ref, gbuf, sem,
                    *, tc, seq_len):
    i = pl.program_id(0)
    cur = i % 2

    def gather(s_ref, half):
        def rows(r, carry):
            for k in range(TOP_K):
                s = s_ref[0, r * TOP_K + k]
                pltpu.make_async_copy(yb_ref.at[s], gbuf.at[half, k, pl.ds(r, 1), :],
                                      sem.at[half]).start()
            return carry
        lax.fori_loop(0, tc, rows, 0)

    @pl.when(i == 0)
    def _():
        gather(slot_ref, 0)

    @pl.when(i + 1 < pl.num_programs(0))
    def _():
        gather(next_slot_ref, 1 - cur)

    for k in range(TOP_K):
        pltpu.make_async_copy(yb_ref.at[pl.ds(0, tc), 0, :], gbuf.at[cur, k], sem.at[cur]).wait()
    tw = tw_ref[...]
    y = tw[:, 0:1] * gbuf[cur, 0]
    for k in range(1, TOP_K):
        y = y + tw[:, k:k + 1] * gbuf[cur, k]
    x2 = x1_ref[...] + _expand_rows(g2_ref, i, tc, seq_len) * y
    y_ref[...] = _rms(x2, gf_ref[...])


def _combine(slots, top_w, x1, g2, gf, yb, seq_len):
    n, d = x1.shape
    n_seq = g2.shape[0]
    tc = _tile(n, 128)
    nt = n // tc
    vmem = 2 * TOP_K * tc * d * 4 + 4 * tc * d * 4 + 6 * tc * d * 4
    slots3 = slots.reshape(nt, 1, tc * TOP_K)
    return pl.pallas_call(
        functools.partial(_combine_kernel, tc=tc, seq_len=seq_len),
        grid=(nt,),
        in_specs=[pl.BlockSpec((None, 1, tc * TOP_K), lambda i: (i, 0, 0), memory_space=pltpu.SMEM),
                  pl.BlockSpec((None, 1, tc * TOP_K), lambda i: (jnp.minimum(i + 1, nt - 1), 0, 0),
                               memory_space=pltpu.SMEM),
                  pl.BlockSpec((tc, TOP_K), lambda i: (i, 0)),
                  pl.BlockSpec((tc, d), lambda i: (i, 0)),
                  _table_spec(n_seq, d, tc, seq_len),
                  pl.BlockSpec((1, d), lambda i: (0, 0)),
                  pl.BlockSpec(memory_space=pl.ANY)],
        out_specs=pl.BlockSpec((tc, d), lambda i: (i, 0)),
        out_shape=jax.ShapeDtypeStruct((n, d), F32),
        scratch_shapes=[pltpu.VMEM((2, TOP_K, tc, d), F32), pltpu.SemaphoreType.DMA((2,))],
        compiler_params=_params(("arbitrary",), vmem),
        name="combine",
    )(slots3, slots3, top_w, x1, g2, gf.reshape(1, d), yb)


def _layer(xp, xs, cp, cs, conv_state, gla_state, w):
    (w_ada, b_ada, norm1_g, w_in, w_dw, b_dw, conv_ln_g, conv_ln_b, w_alpha, b_alpha, gla_norm_g, w_out,
     norm2_g, w_router, b_router, w_gate, b_gate, w_up, b_up, w_down, b_down, final_g) = w
    bp, tp, d = xp.shape
    bs, ts, _ = xs.shape
    cc = w_dw.shape[1]
    qk = w_alpha.shape[1]
    n_exp = w_router.shape[1]
    n_main = 2 * cc + 2 * qk + 2 * cc
    np_, ns_ = bp * tp, bs * ts

    mod = _ada(jnp.concatenate([cp, cs], axis=0), w_ada, b_ada)
    mods_p = [mod[:bp, i * d:(i + 1) * d] for i in range(6)]
    mods_s = [mod[bp:, i * d:(i + 1) * d] for i in range(6)]
    w_in_t = w_in.T
    w_low = w_in_t[n_main:]
    w_main_bf = w_in_t[:n_main].astype(BF16)
    wo_bf = w_out.astype(BF16)

    xp2 = xp.reshape(np_, d)
    xs2 = xs.reshape(ns_, d)

    u_p, la_p = _in_proj(xp2, mods_p[0], mods_p[1], norm1_g, w_main_bf, w_low, w_alpha, b_alpha, tp)
    yc_p, conv_p = _conv_seq(u_p, bp, tp, w_dw, b_dw, conv_ln_g, conv_ln_b)
    yg_p, gla_p = _gla_seq(u_p, la_p, bp, tp, gla_norm_g, cc, qk)
    x1_p, h2_p, ti_p, tw_p = _out_proj(xp2, yc_p, yg_p, wo_bf, mods_p[2], mods_p[3], mods_p[4], norm2_g,
                                       w_router, b_router, tp)

    u_s, la_s = _in_proj(xs2, mods_s[0], mods_s[1], norm1_g, w_main_bf, w_low, w_alpha, b_alpha, ts)
    yc_s, conv_s = _conv_step(u_s, jnp.transpose(conv_state, (0, 2, 1, 3)), ts, w_dw, b_dw, conv_ln_g, conv_ln_b)
    conv_s = jnp.transpose(conv_s, (0, 2, 1, 3))
    yg_s, gla_s = _gla_step(u_s.reshape(bs, ts, n_main), la_s.reshape(bs, ts, qk), gla_state, gla_norm_g, cc, qk)
    x1_s, h2_s, ti_s, tw_s = _out_proj(xs2, yc_s, yg_s.reshape(ns_, cc), wo_bf, mods_s[2],
                                       mods_s[3], mods_s[4], norm2_g, w_router, b_router, ts)

    n_asg = (np_ + ns_) * TOP_K
    slots, cnt, off = _rank(jnp.concatenate([ti_p, ti_s], axis=0), n_exp, MOE_SUB)
    rows = -(-(n_asg + n_exp * MOE_SUB) // MOE_SUB) * MOE_SUB
    xsort = _dispatch(slots, h2_p, h2_s, cnt[0], off[0], rows, MOE_SUB)
    n_chunks = (n_asg + n_exp * MOE_SUB) // MOE_CHUNK + n_exp
    ch_e, ch_row, ch_n, tail = _chunk_table(cnt[0], off[0], n_chunks)
    yb = _moe(xsort, rows, ch_e, ch_row, ch_n, tail, w_gate, b_gate, w_up, b_up, w_down, b_down)

    y_p = _combine(slots[:np_], tw_p, x1_p, mods_p[5], final_g, yb, tp)
    y_s = _combine(slots[np_:], tw_s, x1_s, mods_s[5], final_g, yb, ts)
    return y_p.reshape(bp, tp, d), y_s.reshape(bs, ts, d), conv_p, gla_p, conv_s, gla_s


def kernel(x_prompt, x_sample, c_prompt, c_sample, state_conv, state_gla, w_ada, b_ada, norm1_g, w_in, w_dw, b_dw,
           conv_ln_g, conv_ln_b, w_alpha, b_alpha, gla_norm_g, w_out, norm2_g, w_router, b_router, w_gate, b_gate,
           w_up, b_up, w_down, b_down, final_norm_g):
    assert w_ada.shape[0] == 1, "the final norm is fused into the (single) layer"
    first = lambda a: a.reshape(a.shape[1:])
    w = tuple(first(a) for a in (w_ada, b_ada, norm1_g, w_in, w_dw, b_dw, conv_ln_g, conv_ln_b, w_alpha, b_alpha,
                                 gla_norm_g, w_out, norm2_g, w_router, b_router, w_gate, b_gate, w_up, b_up,
                                 w_down, b_down)) + (final_norm_g,)
    return _layer(x_prompt, x_sample, c_prompt, c_sample, state_conv, state_gla, w)
```

```python
import functools

import numpy as np
import jax
import jax.numpy as jnp
from jax import lax
from jax.experimental import pallas as pl
from jax.experimental.pallas import tpu as pltpu

F32 = jnp.float32
BF16 = jnp.bfloat16
I32 = jnp.int32

EPS = 1e-5
GLA_HEADS = 4
GLA_GATE_NORM = 16.0
GLA_CHUNK = 64
TOP_K = 4
SWIGLU_LIMIT = 7.0
SWIGLU_ALPHA = 1.702

V7X_LANES = 128
V7X_VMEM_BYTES = 64 * 1024 * 1024
VMEM_CAP_BYTES = V7X_VMEM_BYTES - 6 * 1024 * 1024


def _params(semantics, vmem_bytes):
    limit = int(min(VMEM_CAP_BYTES, max(vmem_bytes * 5 // 4 + (4 << 20), 16 << 20)))
    return pltpu.CompilerParams(dimension_semantics=semantics, vmem_limit_bytes=limit)


def _tile(n, want):
    t = min(want, n)
    while n % t or t % 8:
        t -= 8 if t % 8 == 0 else t % 8
        assert t > 0, (n, want)
    return t


NT_DIMS = (((1,), (1,)), ((), ()))


def _dot(a, b):
    return jnp.dot(a, b, preferred_element_type=F32)


def _split3(v):
    hi = v.astype(BF16)
    r = v - hi.astype(F32)
    mid = r.astype(BF16)
    lo = (r - mid.astype(F32)).astype(BF16)
    return hi, mid, lo


def _dot_sel(m01, v):
    hi, mid, lo = _split3(v)
    return _dot(m01, hi) + _dot(m01, mid) + _dot(m01, lo)


def _sigmoid(x):
    return jax.nn.sigmoid(x)


def _rms(x, g):
    return x * lax.rsqrt(jnp.mean(x * x, axis=-1, keepdims=True) + EPS) * g


def _expand_rows(m_ref, i, tm, seq_len):
    if seq_len >= tm:
        return m_ref[pl.ds((i * tm) // seq_len, 1), :]
    nseq = tm // seq_len
    r = lax.broadcasted_iota(I32, (tm, nseq), 0)
    c = lax.broadcasted_iota(I32, (tm, nseq), 1) * seq_len
    sel = ((r >= c) & (r < c + seq_len)).astype(BF16)
    return _dot_sel(sel, m_ref[...])


def _table_spec(n_seq, d, tm, seq_len):
    if seq_len >= tm:
        return pl.BlockSpec((n_seq, d), lambda i, *_: (0, 0))
    return pl.BlockSpec((tm // seq_len, d), lambda i, *_: (i, 0))


def _ada_kernel(c_ref, w_ref, b_ref, o_ref):
    c = c_ref[...]
    s = (c * _sigmoid(c)).astype(BF16)
    o_ref[...] = _dot(s, w_ref[...].astype(BF16)) + b_ref[...]


def _ada(c_all, w_ada, b_ada):
    bc, d = c_all.shape
    n6 = w_ada.shape[1]
    tn = min(1024, n6)
    return pl.pallas_call(
        _ada_kernel,
        grid=(n6 // tn,),
        in_specs=[pl.BlockSpec((bc, d), lambda j: (0, 0)),
                  pl.BlockSpec((d, tn), lambda j: (0, j)),
                  pl.BlockSpec((1, tn), lambda j: (0, j))],
        out_specs=pl.BlockSpec((bc, tn), lambda j: (0, j)),
        out_shape=jax.ShapeDtypeStruct((bc, n6), F32),
        compiler_params=_params(("arbitrary",), 2 * d * tn * 4 + d * tn * 2 + 4 * bc * (d + tn) * 4),
        name="ada",
    )(c_all, w_ada, b_ada.reshape(1, n6))


IN_TN = 512


def _in_kernel(x_ref, sh_ref, sc_ref, g_ref, w_hbm, wlow_ref, wal_ref, bal_ref, u_ref, la_ref, w_vmem, sem,
               *, tm, seq_len):
    i = pl.program_id(0)
    load_w = pltpu.make_async_copy(w_hbm, w_vmem, sem)

    @pl.when(i == 0)
    def _():
        load_w.start()

    xn = _rms(x_ref[...], g_ref[...])
    h = xn * (1.0 + _expand_rows(sc_ref, i, tm, seq_len)) + _expand_rows(sh_ref, i, tm, seq_len)
    hb = h.astype(BF16)
    a_low = lax.dot_general(hb, wlow_ref[...].astype(BF16), NT_DIMS, preferred_element_type=F32)
    z = _dot(a_low.astype(BF16), wal_ref[...].astype(BF16)) + bal_ref[...]
    la_ref[...] = (jnp.minimum(z, 0.0) - jnp.log(1.0 + jnp.exp(-jnp.abs(z)))) * (1.0 / GLA_GATE_NORM)

    @pl.when(i == 0)
    def _():
        load_w.wait()

    for jn in range(u_ref.shape[1] // IN_TN):
        cols = slice(jn * IN_TN, (jn + 1) * IN_TN)
        u_ref[:, cols] = lax.dot_general(hb, w_vmem[cols, :], NT_DIMS, preferred_element_type=F32)


def _in_proj(x2, sh, sc, g, w_main_bf, w_low, w_alpha, b_alpha, seq_len):
    n, d = x2.shape
    n_main = w_main_bf.shape[0]
    n_seq = sh.shape[0]
    qk = w_alpha.shape[1]
    rank = w_low.shape[0]
    tm = _tile(n, 256)
    vmem = d * n_main * 2 + 2 * tm * d * 4 + 2 * tm * n_main * 4 + 2 * tm * qk * 4 + 4 * tm * d * 4
    fix = lambda i: (0, 0)
    return pl.pallas_call(
        functools.partial(_in_kernel, tm=tm, seq_len=seq_len),
        grid=(n // tm,),
        in_specs=[pl.BlockSpec((tm, d), lambda i: (i, 0)),
                  _table_spec(n_seq, d, tm, seq_len),
                  _table_spec(n_seq, d, tm, seq_len),
                  pl.BlockSpec((1, d), fix),
                  pl.BlockSpec(memory_space=pl.ANY),
                  pl.BlockSpec((rank, d), fix),
                  pl.BlockSpec((rank, qk), fix),
                  pl.BlockSpec((1, qk), fix)],
        out_specs=[pl.BlockSpec((tm, n_main), lambda i: (i, 0)),
                   pl.BlockSpec((tm, qk), lambda i: (i, 0))],
        out_shape=[jax.ShapeDtypeStruct((n, n_main), F32), jax.ShapeDtypeStruct((n, qk), F32)],
        scratch_shapes=[pltpu.VMEM((n_main, d), BF16), pltpu.SemaphoreType.DMA],
        compiler_params=_params(("arbitrary",), vmem),
        name="in_proj",
    )(x2, sh, sc, g.reshape(1, d), w_main_bf, w_low, w_alpha, b_alpha.reshape(1, qk))


CONV_HALO = 32


def _ln_silu(y, g, b):
    mu = jnp.mean(y, axis=-1, keepdims=True)
    yc = y - mu
    var = jnp.mean(yc * yc, axis=-1, keepdims=True)
    yn = yc * lax.rsqrt(var + EPS) * g + b
    return yn * _sigmoid(yn)


def _conv_seq_kernel(ua_ref, ub_ref, w_ref, b_ref, lg_ref, lb_ref, y_ref, nb_ref, buf, shifted, *, tt, kw):
    t = pl.program_id(1)
    cc = buf.shape[1]

    @pl.when(t == 0)
    def _():
        buf[0:CONV_HALO, :] = jnp.zeros((CONV_HALO, cc), F32)

    buf[CONV_HALO:CONV_HALO + tt, :] = ua_ref[...] * _sigmoid(ub_ref[...])
    span = shifted.shape[1]
    for r in range(1, 8):
        shifted[r - 1] = buf[r:r + span, :]
    first = CONV_HALO - (kw - 1)

    acc = jnp.broadcast_to(b_ref[...], (tt, cc))
    for j in range(kw):
        q, r = divmod(first + j, 8)
        win = buf[8 * q:8 * q + tt, :] if r == 0 else shifted[r - 1, 8 * q:8 * q + tt, :]
        acc = acc + w_ref[j:j + 1, :] * win
    y_ref[...] = _ln_silu(acc, lg_ref[...], lb_ref[...]).astype(y_ref.dtype)

    @pl.when(t == pl.num_programs(1) - 1)
    def _():
        nb_ref[...] = buf[CONV_HALO + tt - (kw - 1):CONV_HALO + tt, :]

    buf[0:CONV_HALO, :] = buf[tt:tt + CONV_HALO, :]


def _conv_seq(u2, n_seq, seq_len, w_dw, b_dw, ln_g, ln_b):
    kw, cc = w_dw.shape
    assert kw - 1 <= CONV_HALO
    tt = _tile(seq_len, 256)
    assert tt % 8 == 0 and tt >= CONV_HALO
    nt = seq_len // tt
    span = tt + CONV_HALO - 8
    row = lambda v: v.reshape(1, cc)
    vmem = 4 * tt * cc * 4 + 2 * tt * cc * 2 + (CONV_HALO + tt + 7 * span) * cc * 4 + 6 * tt * cc * 4
    return pl.pallas_call(
        functools.partial(_conv_seq_kernel, tt=tt, kw=kw),
        grid=(n_seq, nt),
        in_specs=[pl.BlockSpec((tt, cc), lambda b, t: (b * nt + t, 0)),
                  pl.BlockSpec((tt, cc), lambda b, t: (b * nt + t, 1)),
                  pl.BlockSpec((kw, cc), lambda b, t: (0, 0)),
                  pl.BlockSpec((1, cc), lambda b, t: (0, 0)),
                  pl.BlockSpec((1, cc), lambda b, t: (0, 0)),
                  pl.BlockSpec((1, cc), lambda b, t: (0, 0))],
        out_specs=[pl.BlockSpec((tt, cc), lambda b, t: (b * nt + t, 0)),
                   pl.BlockSpec((None, None, kw - 1, cc), lambda b, t: (0, b, 0, 0))],
        out_shape=[jax.ShapeDtypeStruct((n_seq * seq_len, cc), BF16),
                   jax.ShapeDtypeStruct((1, n_seq, kw - 1, cc), F32)],
        scratch_shapes=[pltpu.VMEM((CONV_HALO + tt, cc), F32), pltpu.VMEM((7, span, cc), F32)],
        compiler_params=_params(("arbitrary", "arbitrary"), vmem),
        name="conv_seq",
    )(u2, u2, w_dw, row(b_dw), row(ln_g), row(ln_b))


def _conv_step_kernel(ua_ref, ub_ref, st_ref, w_ref, b_ref, lg_ref, lb_ref, y_ref, nb_ref, gbuf, ybuf,
                      *, kw, t_len):
    hist = kw - 1
    n_seq, cc = st_ref.shape[1], st_ref.shape[2]
    n_lt = gbuf.shape[0]
    lanes = gbuf.shape[2]
    glu = ua_ref[...] * _sigmoid(ub_ref[...])
    for lt in range(n_lt):
        gbuf[lt] = glu[:, lt * lanes:(lt + 1) * lanes]

    def slab(r):
        if r < hist:
            return st_ref[r]
        step = pl.ds(r - hist, n_seq, stride=t_len)
        return jnp.concatenate([gbuf[lt, step, :] for lt in range(n_lt)], axis=1)

    for t in range(t_len):
        acc = jnp.broadcast_to(b_ref[...], (n_seq, cc))
        for j in range(kw):
            acc = acc + w_ref[j:j + 1, :] * slab(t + j)
        y = _ln_silu(acc, lg_ref[...], lb_ref[...])
        for lt in range(n_lt):
            ybuf[lt, pl.ds(t, n_seq, stride=t_len), :] = y[:, lt * lanes:(lt + 1) * lanes]
    y_ref[...] = jnp.concatenate([ybuf[lt] for lt in range(n_lt)], axis=1)
    for r in range(hist):
        nb_ref[r] = slab(r + t_len)


def _conv_step(u2, state_t, t_len, w_dw, b_dw, ln_g, ln_b):
    kw, cc = w_dw.shape
    n_seq = state_t.shape[2]
    bs = _tile(n_seq, 16)
    tb = bs * t_len
    row = lambda v: v.reshape(1, cc)
    vmem = 4 * tb * cc * 4 + 2 * tb * cc * 4 + 4 * (kw - 1) * bs * cc * 4 + tb * cc * 4 + 4 * tb * cc * 4
    hist_spec = pl.BlockSpec((None, kw - 1, bs, cc), lambda i: (0, 0, i, 0))
    return pl.pallas_call(
        functools.partial(_conv_step_kernel, kw=kw, t_len=t_len),
        grid=(n_seq // bs,),
        in_specs=[pl.BlockSpec((tb, cc), lambda i: (i, 0)),
                  pl.BlockSpec((tb, cc), lambda i: (i, 1)),
                  hist_spec,
                  pl.BlockSpec((kw, cc), lambda i: (0, 0)),
                  pl.BlockSpec((1, cc), lambda i: (0, 0)),
                  pl.BlockSpec((1, cc), lambda i: (0, 0)),
                  pl.BlockSpec((1, cc), lambda i: (0, 0))],
        out_specs=[pl.BlockSpec((tb, cc), lambda i: (i, 0)), hist_spec],
        out_shape=[jax.ShapeDtypeStruct((n_seq * t_len, cc), F32),
                   jax.ShapeDtypeStruct(state_t.shape, F32)],
        scratch_shapes=[pltpu.VMEM((cc // V7X_LANES, tb, V7X_LANES), F32),
                        pltpu.VMEM((cc // V7X_LANES, tb, V7X_LANES), F32)],
        compiler_params=_params(("arbitrary",), vmem),
        name="conv_step",
    )(u2, u2, state_t, w_dw, row(b_dw), row(ln_g), row(ln_b))


def _gla_consts(c):
    n_lev = int(np.log2(c))
    i = np.arange(c)
    mall = [(i[None, :] <= i[:, None])]
    masks = []
    for lev in range(n_lev):
        s = 1 << lev
        m = (i // (2 * s)) * (2 * s) + s - 1
        mall.append(i[None, :] <= m[:, None])
        same = (i[:, None] // (2 * s)) == (i[None, :] // (2 * s))
        upper = ((i[:, None] // s) % 2) == 1
        lower = ((i[None, :] // s) % 2) == 0
        masks.append(same & upper & lower)
    mall = jnp.asarray(np.concatenate(mall, axis=0).astype(np.float32), dtype=BF16)
    masks = jnp.asarray(np.stack(masks).astype(np.float32))
    return mall, masks, n_lev


def _gla_intra(units, mask_ref, c, n_lev):
    nt_dims = (((1,), (1,)), ((), ()))
    tn_dims = (((0,), (0,)), ((), ()))
    staged = []
    for q, k, v, ball in units:
        bc = ball[0:c]
        b_last = bc[c - 1:c, :]
        levels = []
        for lev in range(n_lev):
            bm = ball[(1 + lev) * c:(2 + lev) * c]
            decay = jnp.exp(-jnp.abs(bc - bm))
            levels.append(((q * decay).astype(BF16), (k * decay).astype(BF16)))
        staged.append(dict(qe=(q * jnp.exp(bc)).astype(BF16), kd=(k * jnp.exp(b_last - bc)).astype(BF16),
                           vb=v.astype(BF16), diag=jnp.sum(q * k, axis=-1, keepdims=True) * v,
                           b_last=b_last, levels=levels))
    atts = []
    for u in staged:
        att = None
        for lev, (ql, kl) in enumerate(u["levels"]):
            a = lax.dot_general(ql, kl, nt_dims, preferred_element_type=F32)
            a = jnp.where(mask_ref[lev] > 0.5, a, 0.0)
            att = a if att is None else att + a
        atts.append(att)
    out = []
    for u, att in zip(staged, atts):
        dk = u["kd"].shape[1]
        dv = u["vb"].shape[1]
        o_intra = _dot(att.astype(BF16), u["vb"]) + u["diag"]
        upd = lax.dot_general(u["kd"], u["vb"], tn_dims, preferred_element_type=F32)
        dec = jnp.exp(jnp.transpose(jnp.broadcast_to(u["b_last"], (dk, dk))))
        out.append((u["qe"], o_intra, upd, jnp.tile(dec, (1, dv // dk))))
    return out


def _gla_finish(o, g, gn):
    return _rms(o, gn) * (g * _sigmoid(g))


GLA_SEQ_CHUNKS = 4
GLA_STEP_SEQS = 4


def _gla_seq_kernel(q_ref, k_ref, v_ref, g_ref, la_ref, gn_ref, mall_ref, mask_ref, y_ref, sf_ref, s_scr,
                    *, c, n_lev, heads):
    t = pl.program_id(1)
    dk = q_ref.shape[1] // heads
    dv = v_ref.shape[1] // heads
    n_ch = q_ref.shape[0] // c

    @pl.when(t == 0)
    def _():
        s_scr[...] = jnp.zeros(s_scr.shape, F32)

    mall = mall_ref[...]
    units = []
    for ci in range(n_ch):
        rows = slice(ci * c, (ci + 1) * c)
        ball = _dot_sel(mall, la_ref[rows, :])
        for h in range(heads):
            ks = slice(h * dk, (h + 1) * dk)
            units.append((q_ref[rows, ks] * (dk ** -0.5), k_ref[rows, ks], v_ref[rows, h * dv:(h + 1) * dv],
                          ball[:, ks]))
    parts = _gla_intra(units, mask_ref, c, n_lev)
    state = [s_scr[h] for h in range(heads)]
    for ci in range(n_ch):
        rows = slice(ci * c, (ci + 1) * c)
        for h in range(heads):
            vs = slice(h * dv, (h + 1) * dv)
            qe, o_intra, upd, dec = parts[ci * heads + h]
            o = o_intra + _dot(qe, state[h].astype(BF16))
            state[h] = state[h] * dec + upd
            y_ref[rows, vs] = _gla_finish(o, g_ref[rows, vs], gn_ref[...]).astype(y_ref.dtype)
    for h in range(heads):
        s_scr[h] = state[h]

    @pl.when(t == pl.num_programs(1) - 1)
    def _():
        sf_ref[...] = s_scr[...]


def _gla_seq(u2, la, n_seq, seq_len, gn_g, cc, qk):
    heads = GLA_HEADS
    dk = qk // heads
    dv = cc // heads
    c = int(np.gcd(seq_len, GLA_CHUNK))
    n_ch = GLA_SEQ_CHUNKS if (seq_len // c) % GLA_SEQ_CHUNKS == 0 else 1
    tb = n_ch * c
    nt = seq_len // tb
    mall, masks, n_lev = _gla_consts(c)
    qb = 2 * cc // qk
    vb = (2 * cc + 2 * qk) // cc
    vmem = 2 * tb * (3 * qk + 3 * cc) * 4 + 3 * heads * dk * dv * 4 + 96 * n_ch * heads * c * max(dv, 128) * 4
    return pl.pallas_call(
        functools.partial(_gla_seq_kernel, c=c, n_lev=n_lev, heads=heads),
        grid=(n_seq, nt),
        in_specs=[pl.BlockSpec((tb, qk), lambda b, t: (b * nt + t, qb)),
                  pl.BlockSpec((tb, qk), lambda b, t: (b * nt + t, qb + 1)),
                  pl.BlockSpec((tb, cc), lambda b, t: (b * nt + t, vb)),
                  pl.BlockSpec((tb, cc), lambda b, t: (b * nt + t, vb + 1)),
                  pl.BlockSpec((tb, qk), lambda b, t: (b * nt + t, 0)),
                  pl.BlockSpec((1, dv), lambda b, t: (0, 0)),
                  pl.BlockSpec(mall.shape, lambda b, t: (0, 0)),
                  pl.BlockSpec(masks.shape, lambda b, t: (0, 0, 0))],
        out_specs=[pl.BlockSpec((tb, cc), lambda b, t: (b * nt + t, 0)),
                   pl.BlockSpec((None, None, heads, dk, dv), lambda b, t: (0, b, 0, 0, 0))],
        out_shape=[jax.ShapeDtypeStruct((n_seq * seq_len, cc), BF16),
                   jax.ShapeDtypeStruct((1, n_seq, heads, dk, dv), F32)],
        scratch_shapes=[pltpu.VMEM((heads, dk, dv), F32)],
        compiler_params=_params(("arbitrary", "arbitrary"), vmem),
        name="gla_seq",
    )(u2, u2, u2, u2, la, gn_g.reshape(1, dv), mall, masks)


def _gla_step_kernel(q_ref, k_ref, v_ref, g_ref, la_ref, st_ref, gn_ref, mall_ref, mask_ref, y_ref, sn_ref,
                     *, c, n_lev, heads):
    dk = q_ref.shape[2] // heads
    dv = v_ref.shape[2] // heads
    mall = mall_ref[...]

    n_un = GLA_STEP_SEQS if q_ref.shape[0] % GLA_STEP_SEQS == 0 else 1

    def body(t, carry):
        units = []
        for ss in range(n_un):
            s = t * n_un + ss
            ball = _dot_sel(mall, la_ref[s])
            for h in range(heads):
                ks = slice(h * dk, (h + 1) * dk)
                units.append((q_ref[s, :, ks] * (dk ** -0.5), k_ref[s, :, ks], v_ref[s, :, h * dv:(h + 1) * dv],
                              ball[:, ks]))
        parts = _gla_intra(units, mask_ref, c, n_lev)
        for ss in range(n_un):
            s = t * n_un + ss
            for h in range(heads):
                vs = slice(h * dv, (h + 1) * dv)
                qe, o_intra, upd, dec = parts[ss * heads + h]
                s0 = st_ref[s, h]
                o = o_intra + _dot(qe, s0.astype(BF16))
                sn_ref[s, h] = s0 * dec + upd
                y_ref[s, :, vs] = _gla_finish(o, g_ref[s, :, vs], gn_ref[...])
        return carry

    lax.fori_loop(0, q_ref.shape[0] // n_un, body, 0)


def _gla_step(u3, la3, state, gn_g, cc, qk):
    n_seq, t_len, _ = u3.shape
    heads = GLA_HEADS
    dk = qk // heads
    dv = cc // heads
    assert t_len <= GLA_CHUNK and GLA_CHUNK % t_len == 0, "single-chunk path"
    mall, masks, n_lev = _gla_consts(t_len)
    bs = min(8, n_seq)
    qb = 2 * cc // qk
    vb = (2 * cc + 2 * qk) // cc
    vmem = 2 * bs * t_len * (3 * qk + 3 * cc) * 4 + 4 * bs * heads * dk * dv * 4 + 64 * 8 * max(dv, 128) * 4
    return pl.pallas_call(
        functools.partial(_gla_step_kernel, c=t_len, n_lev=n_lev, heads=heads),
        grid=(n_seq // bs,),
        in_specs=[pl.BlockSpec((bs, t_len, qk), lambda i: (i, 0, qb)),
                  pl.BlockSpec((bs, t_len, qk), lambda i: (i, 0, qb + 1)),
                  pl.BlockSpec((bs, t_len, cc), lambda i: (i, 0, vb)),
                  pl.BlockSpec((bs, t_len, cc), lambda i: (i, 0, vb + 1)),
                  pl.BlockSpec((bs, t_len, qk), lambda i: (i, 0, 0)),
                  pl.BlockSpec((None, bs, heads, dk, dv), lambda i: (0, i, 0, 0, 0)),
                  pl.BlockSpec((1, dv), lambda i: (0, 0)),
                  pl.BlockSpec(mall.shape, lambda i: (0, 0)),
                  pl.BlockSpec(masks.shape, lambda i: (0, 0, 0))],
        out_specs=[pl.BlockSpec((bs, t_len, cc), lambda i: (i, 0, 0)),
                   pl.BlockSpec((None, bs, heads, dk, dv), lambda i: (0, i, 0, 0, 0))],
        out_shape=[jax.ShapeDtypeStruct((n_seq, t_len, cc), F32),
                   jax.ShapeDtypeStruct((1, n_seq, heads, dk, dv), F32)],
        compiler_params=_params(("arbitrary",), vmem),
        name="gla_step",
    )(u3, u3, u3, u3, la3, state, gn_g.reshape(1, dv), mall, masks)


def _out_kernel(x_ref, yc_ref, yg_ref, wo_ref, g1_ref, sh_ref, sc_ref, n2_ref, wr_ref, br_ref,
                x1_ref, h2_ref, ti_ref, tw_ref, *, tm, seq_len):
    i = pl.program_id(0)
    cc = yc_ref.shape[1]
    n_exp = wr_ref.shape[1]
    mix = _dot(yc_ref[...].astype(BF16), wo_ref[0:cc, :]) + _dot(yg_ref[...].astype(BF16), wo_ref[cc:, :])
    x1 = x_ref[...] + _expand_rows(g1_ref, i, tm, seq_len) * mix
    x1_ref[...] = x1
    h2 = _rms(x1, n2_ref[...]) * (1.0 + _expand_rows(sc_ref, i, tm, seq_len)) + _expand_rows(sh_ref, i, tm, seq_len)
    h2_ref[...] = h2

    logits = _dot(h2.astype(BF16), wr_ref[...].astype(BF16)) + br_ref[...]
    lane = lax.broadcasted_iota(I32, logits.shape, 1)
    vals = []
    for r in range(TOP_K):
        m = jnp.max(logits, axis=-1, keepdims=True)
        idx = jnp.min(jnp.where(logits == m, lane, n_exp), axis=-1, keepdims=True)
        vals.append(m)
        ti_ref[:, r:r + 1] = idx
        logits = jnp.where(lane == idx, -jnp.inf, logits)
    ex = [jnp.exp(v - vals[0]) for v in vals]
    den = ex[0] + ex[1] + ex[2] + ex[3]
    for r in range(TOP_K):
        tw_ref[:, r:r + 1] = ex[r] / den


def _out_proj(x2, yc, yg, wo_bf, g1, sh2, sc2, n2, w_router, b_router, seq_len):
    n, d = x2.shape
    cc = yc.shape[1]
    n_seq = g1.shape[0]
    n_exp = w_router.shape[1]
    tm = _tile(n, 256)
    vmem = (6 * tm * d * 4 + 4 * tm * cc * 4 + 2 * 2 * cc * d * 2 + 8 * tm * d * 4)
    tok = lambda i: (i, 0)
    fix = lambda i: (0, 0)
    return pl.pallas_call(
        functools.partial(_out_kernel, tm=tm, seq_len=seq_len),
        grid=(n // tm,),
        in_specs=[pl.BlockSpec((tm, d), tok),
                  pl.BlockSpec((tm, cc), tok),
                  pl.BlockSpec((tm, cc), tok),
                  pl.BlockSpec((2 * cc, d), fix),
                  _table_spec(n_seq, d, tm, seq_len),
                  _table_spec(n_seq, d, tm, seq_len),
                  _table_spec(n_seq, d, tm, seq_len),
                  pl.BlockSpec((1, d), fix),
                  pl.BlockSpec((d, n_exp), fix),
                  pl.BlockSpec((1, n_exp), fix)],
        out_specs=[pl.BlockSpec((tm, d), tok),
                   pl.BlockSpec((tm, d), tok),
                   pl.BlockSpec((tm, TOP_K), tok),
                   pl.BlockSpec((tm, TOP_K), tok)],
        out_shape=[jax.ShapeDtypeStruct((n, d), F32),
                   jax.ShapeDtypeStruct((n, d), F32),
                   jax.ShapeDtypeStruct((n, TOP_K), I32),
                   jax.ShapeDtypeStruct((n, TOP_K), F32)],
        compiler_params=_params(("arbitrary",), vmem),
        name="out_proj",
    )(x2, yc, yg, wo_bf, g1, sh2, sc2, n2.reshape(1, d), w_router, b_router.reshape(1, n_exp))


def _rank_kernel(ti_ref, slot_ref, cnt_ref, off_ref, carry, offs, *, tr, n_exp, group):
    p = pl.program_id(0)
    i = pl.program_id(1)
    lane = lax.broadcasted_iota(I32, (tr, n_exp), 1)
    ti = ti_ref[...]
    hot = jnp.zeros((tr, n_exp), F32)
    for k in range(TOP_K):
        hot = hot + (lane == ti[:, k:k + 1]).astype(F32)

    @pl.when((p == 0) & (i == 0))
    def _():
        carry[...] = jnp.zeros(carry.shape, F32)

    @pl.when((p == 1) & (i == 0))
    def _():
        cnt = carry[0:1, :]
        padded = jnp.ceil(cnt * (1.0 / group)) * group
        r = lax.broadcasted_iota(I32, (n_exp, n_exp), 0)
        c = lax.broadcasted_iota(I32, (n_exp, n_exp), 1)
        before = (r < c).astype(BF16)
        hi, mid, lo = _split3(jnp.broadcast_to(padded, (8, n_exp)))
        off = _dot(hi, before) + _dot(mid, before) + _dot(lo, before)
        offs[...] = off
        cnt_ref[...] = cnt.astype(I32)
        off_ref[...] = off[0:1, :].astype(I32)
        carry[...] = jnp.zeros(carry.shape, F32)

    @pl.when(p == 1)
    def _():
        r = lax.broadcasted_iota(I32, (tr, tr), 0)
        c = lax.broadcasted_iota(I32, (tr, tr), 1)
        earlier = (c < r).astype(BF16)
        tot = _dot(earlier, hot.astype(BF16)) + carry[0:1, :] + offs[0:1, :]
        for k in range(TOP_K):
            sel = jnp.where(lane == ti[:, k:k + 1], tot, 0.0)
            slot_ref[:, k:k + 1] = jnp.sum(sel, axis=-1, keepdims=True).astype(I32)

    carry[...] = carry[...] + jnp.sum(hot, axis=0, keepdims=True)


def _rank(top_i, n_exp, group):
    n = top_i.shape[0]
    tr = _tile(n, 512)
    return pl.pallas_call(
        functools.partial(_rank_kernel, tr=tr, n_exp=n_exp, group=group),
        grid=(2, n // tr),
        in_specs=[pl.BlockSpec((tr, TOP_K), lambda p, i: (i, 0))],
        out_specs=[pl.BlockSpec((tr, TOP_K), lambda p, i: (i * p, 0)),
                   pl.BlockSpec((1, n_exp), lambda p, i: (0, 0)),
                   pl.BlockSpec((1, n_exp), lambda p, i: (0, 0))],
        out_shape=[jax.ShapeDtypeStruct((n, TOP_K), I32),
                   jax.ShapeDtypeStruct((1, n_exp), I32),
                   jax.ShapeDtypeStruct((1, n_exp), I32)],
        scratch_shapes=[pltpu.VMEM((8, n_exp), F32), pltpu.VMEM((8, n_exp), F32)],
        compiler_params=_params(("arbitrary", "arbitrary"), 8 * tr * tr * 4),
        name="rank",
    )(top_i)


def _scatter_rows(slot_ref, h_ref, xs_ref, sem, td):
    def issue(r, carry):
        for k in range(TOP_K):
            s = slot_ref[0, r * TOP_K + k]
            pltpu.make_async_copy(h_ref.at[pl.ds(r, 1), :], xs_ref.at[s], sem).start()
        return carry

    lax.fori_loop(0, td, issue, 0)
    for k in range(TOP_K):
        pltpu.make_async_copy(h_ref, xs_ref.at[pl.ds(0, td), 0, :], sem).wait()


def _dispatch_kernel(cnt, off, slot_ref, ha_ref, hb_ref, xs_ref, zbuf, sem, zsem, *, td, group, n_a):
    n_exp = cnt.shape[0]
    n_rows = xs_ref.shape[0]
    i = pl.program_id(0)

    def padded(e):
        return (cnt[e] + group - 1) // group * group

    def z_row(row):
        return pltpu.make_async_copy(zbuf.at[pl.ds(0, 1), :], xs_ref.at[row], zsem)

    def z_block(row):
        return pltpu.make_async_copy(zbuf, xs_ref.at[pl.ds(pl.multiple_of(row, group), group), 0, :], zsem)

    def for_pad_rows(fn):
        def expert(e, carry):
            lax.fori_loop(off[e] + cnt[e], off[e] + padded(e), lambda r, c: (fn(r), c)[1], 0)
            return carry
        lax.fori_loop(0, n_exp, expert, 0)

    def for_tail_blocks(fn):
        first = (off[n_exp - 1] + padded(n_exp - 1)) // group
        lax.fori_loop(first, n_rows // group, lambda b, c: (fn(b * group), c)[1], 0)

    @pl.when(i == 0)
    def _():
        zbuf[...] = jnp.zeros(zbuf.shape, zbuf.dtype)
        for_pad_rows(lambda r: z_row(r).start())
        for_tail_blocks(lambda r: z_block(r).start())

    @pl.when(i < n_a)
    def _():
        _scatter_rows(slot_ref, ha_ref, xs_ref, sem, td)

    @pl.when(i >= n_a)
    def _():
        _scatter_rows(slot_ref, hb_ref, xs_ref, sem, td)

    @pl.when(i == pl.num_programs(0) - 1)
    def _():
        for_pad_rows(lambda r: z_row(0).wait())
        for_tail_blocks(lambda r: z_block(0).wait())


def _dispatch(slots, h_a, h_b, cnt, off, n_rows, group):
    d = h_a.shape[1]
    td = _tile(int(np.gcd(h_a.shape[0], h_b.shape[0])), 512)
    n_a, n_b = h_a.shape[0] // td, h_b.shape[0] // td
    grid_spec = pltpu.PrefetchScalarGridSpec(
        num_scalar_prefetch=2,
        grid=(n_a + n_b,),
        in_specs=[pl.BlockSpec((None, 1, td * TOP_K), lambda i, *_: (i, 0, 0), memory_space=pltpu.SMEM),
                  pl.BlockSpec((td, d), lambda i, *_: (jnp.minimum(i, n_a - 1), 0)),
                  pl.BlockSpec((td, d), lambda i, *_: (jnp.maximum(i - n_a, 0), 0))],
        out_specs=pl.BlockSpec(memory_space=pl.ANY),
        scratch_shapes=[pltpu.VMEM((group, d), h_a.dtype), pltpu.SemaphoreType.DMA, pltpu.SemaphoreType.DMA])
    return pl.pallas_call(
        functools.partial(_dispatch_kernel, td=td, group=group, n_a=n_a),
        grid_spec=grid_spec,
        out_shape=jax.ShapeDtypeStruct((n_rows, 1, d), h_a.dtype),
        compiler_params=_params(("arbitrary",), 4 * td * d * 4 + group * d * 4),
        name="dispatch",
    )(cnt, off, slots.reshape(n_a + n_b, 1, td * TOP_K), h_a, h_b)


MOE_CHUNK = 1536
MOE_SUB = 128
MOE_MM = 2
MOE_TF = 256
MOE_TN = 512


def _for_row_blocks(n_units, body):
    n = n_units // MOE_MM

    def block(b):
        body(b * MOE_MM, MOE_MM)

    def trip(t, carry):
        for u in range(4):
            block(t * 4 + u)
        return carry

    lax.fori_loop(0, n // 4, trip, 0)

    @pl.when((n // 2) % 2 == 1)
    def _():
        block((n // 4) * 4)
        block((n // 4) * 4 + 1)

    @pl.when(n % 2 == 1)
    def _():
        block(n - 1)

    @pl.when(n_units % MOE_MM == 1)
    def _():
        body(n_units - 1, 1)


def _moe_kernel(che, chrow, chn, tail, xs_ref, wg_ref, wu_ref, bg_ref, bu_ref, wd_ref, bd_ref, yb_ref,
                xbuf, xb, act, wgb, wub, wdb, obuf, semx, semo, *, n_f, n_n, n_chunks):
    c = pl.program_id(0)
    j = pl.program_id(1)
    nsb = chn[c]
    sub = MOE_SUB

    def x_copy(cc, i):
        row = pl.multiple_of(chrow[cc] + i * sub, sub)
        return pltpu.make_async_copy(xs_ref.at[pl.ds(row, sub), 0, :],
                                     xbuf.at[pl.ds(pl.multiple_of(i * sub, sub), sub), :], semx)

    def x_fetch(cc):
        lax.fori_loop(0, chn[cc], lambda i, carry: (x_copy(cc, i).start(), carry)[1], 0)

    def o_copy(slot, i, jj):
        row = pl.multiple_of(chrow[c] + i * sub, sub)
        col = pl.multiple_of(jj * MOE_TN, MOE_TN)
        return pltpu.make_async_copy(obuf.at[slot, pl.ds(pl.multiple_of(i * sub, sub), sub), :],
                                     yb_ref.at[pl.ds(row, sub), 0, pl.ds(col, MOE_TN)], semo.at[slot])

    @pl.when((c == 0) & (j == 0))
    def _():
        x_fetch(0)

    @pl.when(j == 0)
    def _():
        lax.fori_loop(0, nsb, lambda i, carry: (x_copy(c, 0).wait(), carry)[1], 0)

        def cast(i, carry):
            rows = pl.ds(pl.multiple_of(i * sub, sub), sub)
            xb[rows, :] = xbuf[rows, :].astype(BF16)
            return carry

        lax.fori_loop(0, nsb, cast, 0)

    @pl.when((j == 1) & (c + 1 < n_chunks))
    def _():
        x_fetch(jnp.minimum(c + 1, n_chunks - 1))

    @pl.when((j < n_f) & (nsb > 0))
    def _():
        wgb[...] = wg_ref[...].astype(BF16)
        wub[...] = wu_ref[...].astype(BF16)
        tile_row = pl.ds(che[c] * n_f + j, 1)
        bg = bg_ref[tile_row, :]
        bu = bu_ref[tile_row, :]

        def row_block(u, n):
            rows = pl.ds(pl.multiple_of(u * sub, sub), n * sub)
            x = xb[rows, :]
            gate = jnp.minimum(_dot(x, wgb[...]) + bg, SWIGLU_LIMIT)
            up = jnp.clip(_dot(x, wub[...]) + bu, -SWIGLU_LIMIT, SWIGLU_LIMIT)
            glu = gate * _sigmoid(SWIGLU_ALPHA * gate)
            act[j, rows, :] = ((up + 1.0) * glu).astype(BF16)

        _for_row_blocks(nsb, row_block)

    @pl.when((j >= n_f) & (nsb > 0))
    def _():
        jj = j - n_f
        slot = jj % 2
        wdb[...] = wd_ref[...].astype(BF16)
        bd = bd_ref[pl.ds(che[c] * n_n + jj, 1), :]

        def retire(s):
            def one(i, carry):
                o_copy(s, 0, 0).wait()
                return carry
            lax.fori_loop(0, nsb, one, 0)

        @pl.when(jj >= 2)
        def _():
            retire(slot)

        def row_block(u, n):
            rows = pl.ds(pl.multiple_of(u * sub, sub), n * sub)
            a = jnp.concatenate([act[f, rows, :] for f in range(n_f)], axis=1)
            obuf[slot, rows, :] = _dot(a, wdb[...]) + bd
            for k in range(n):
                o_copy(slot, u + k, jj).start()

        _for_row_blocks(nsb, row_block)

        @pl.when(jj == n_n - 1)
        def _():
            retire(slot)
            if n_n >= 2:
                retire(1 - slot)

    @pl.when((c == n_chunks - 1) & (j == n_f + n_n - 1))
    def _():
        n_tail = (yb_ref.shape[0] - tail[0]) // sub
        obuf[0, 0:sub, :] = jnp.zeros((sub, MOE_TN), F32)

        def z_copy(i, js):
            row = pl.multiple_of(tail[0] + i * sub, sub)
            return pltpu.make_async_copy(obuf.at[0, pl.ds(0, sub), :],
                                         yb_ref.at[pl.ds(row, sub), 0, pl.ds(js * MOE_TN, MOE_TN)], semo.at[0])

        def fill(i, carry):
            for js in range(n_n):
                z_copy(i, js).start()
            return carry

        def drain(i, carry):
            for js in range(n_n):
                z_copy(0, js).wait()
            return carry

        lax.fori_loop(0, n_tail, fill, 0)
        lax.fori_loop(0, n_tail, drain, 0)


def _moe(xs, yb_rows, ch_e, ch_row, ch_n, tail, w_gate, b_gate, w_up, b_up, w_down, b_down):
    n_exp, d, f = w_gate.shape
    n_f = f // MOE_TF
    n_n = d // MOE_TN
    n_chunks = ch_e.shape[0]

    def w1(c, j, che, chrow, chn, tail):
        return (che[c], 0, jnp.where(chn[c] > 0, jnp.minimum(j, n_f - 1), n_f - 1))

    def w2(c, j, che, chrow, chn, tail):
        return (che[c], 0, jnp.where(chn[c] > 0, jnp.maximum(j - n_f, 0), n_n - 1))

    def fixed(c, j, che, chrow, chn, tail):
        return (0, 0)

    vmem = (MOE_CHUNK * d * 4 + MOE_CHUNK * d * 2 + MOE_CHUNK * f * 2
            + 2 * 2 * d * MOE_TF * 4 + 2 * f * MOE_TN * 4 + 2 * d * MOE_TF * 2 + f * MOE_TN * 2
            + 2 * MOE_CHUNK * MOE_TN * 4 + 8 * MOE_MM * MOE_SUB * max(d, f) * 4)
    grid_spec = pltpu.PrefetchScalarGridSpec(
        num_scalar_prefetch=4,
        grid=(n_chunks, n_f + n_n),
        in_specs=[pl.BlockSpec(memory_space=pl.ANY),
                  pl.BlockSpec((None, d, MOE_TF), w1),
                  pl.BlockSpec((None, d, MOE_TF), w1),
                  pl.BlockSpec((n_exp * n_f, MOE_TF), fixed),
                  pl.BlockSpec((n_exp * n_f, MOE_TF), fixed),
                  pl.BlockSpec((None, f, MOE_TN), w2),
                  pl.BlockSpec((n_exp * n_n, MOE_TN), fixed)],
        out_specs=pl.BlockSpec(memory_space=pl.ANY),
        scratch_shapes=[pltpu.VMEM((MOE_CHUNK, d), F32),
                        pltpu.VMEM((MOE_CHUNK, d), BF16),
                        pltpu.VMEM((n_f, MOE_CHUNK, MOE_TF), BF16),
                        pltpu.VMEM((d, MOE_TF), BF16),
                        pltpu.VMEM((d, MOE_TF), BF16),
                        pltpu.VMEM((f, MOE_TN), BF16),
                        pltpu.VMEM((2, MOE_CHUNK, MOE_TN), F32),
                        pltpu.SemaphoreType.DMA,
                        pltpu.SemaphoreType.DMA((2,))])
    return pl.pallas_call(
        functools.partial(_moe_kernel, n_f=n_f, n_n=n_n, n_chunks=n_chunks),
        grid_spec=grid_spec,
        out_shape=jax.ShapeDtypeStruct((yb_rows, 1, d), F32),
        compiler_params=_params(("arbitrary", "arbitrary"), vmem),
        name="moe",
    )(ch_e, ch_row, ch_n, tail, xs, w_gate, w_up, b_gate.reshape(n_exp * n_f, MOE_TF),
      b_up.reshape(n_exp * n_f, MOE_TF), w_down, b_down.reshape(n_exp * n_n, MOE_TN))


def _chunk_table(cnt, off, n_chunks):
    n_exp = cnt.shape[0]
    per = MOE_CHUNK // MOE_SUB
    nsb = (cnt + MOE_SUB - 1) // MOE_SUB
    nch = (nsb + per - 1) // per
    end = jnp.cumsum(nch)
    start = end - nch
    cidx = jnp.arange(n_chunks, dtype=I32)
    owner = lambda ci: jnp.minimum(jnp.sum((end[None, :] <= ci[:, None]).astype(I32), axis=1), n_exp - 1)
    e = owner(cidx)
    used = cidx < end[-1]
    last_e = owner(end[-1:] - 1)[0]
    local = cidx - start[e]
    ch_e = jnp.where(used, e, last_e)
    ch_row = jnp.where(used, off[e] + local * MOE_CHUNK, 0).astype(I32)
    ch_n = jnp.where(used, jnp.minimum(per, nsb[e] - local * per), 0).astype(I32)
    tail = (off[-1] + nsb[-1] * MOE_SUB).astype(I32).reshape(1)
    return ch_e, ch_row, ch_n, tail


def _combine_kernel(slot_ref, next_slot_ref, tw_ref, x1_ref, g2_ref, gf_ref, yb_ref, y_ref, gbuf, sem,
                    *, tc, seq_len):
    i = pl.program_id(0)
    cur = i % 2

    def gather(s_ref, half):
        def rows(r, carry):
            for k in range(TOP_K):
                s = s_ref[0, r * TOP_K + k]
                pltpu.make_async_copy(yb_ref.at[s], gbuf.at[half, k, pl.ds(r, 1), :],
                                      sem.at[half]).start()
            return carry
        lax.fori_loop(0, tc, rows, 0)

    @pl.when(i == 0)
    def _():
        gather(slot_ref, 0)

    @pl.when(i + 1 < pl.num_programs(0))
    def _():
        gather(next_slot_ref, 1 - cur)

    for k in range(TOP_K):
        pltpu.make_async_copy(yb_ref.at[pl.ds(0, tc), 0, :], gbuf.at[cur, k], sem.at[cur]).wait()
    tw = tw_ref[...]
    y = tw[:, 0:1] * gbuf[cur, 0]
    for k in range(1, TOP_K):
        y = y + tw[:, k:k + 1] * gbuf[cur, k]
    x2 = x1_ref[...] + _expand_rows(g2_ref, i, tc, seq_len) * y
    y_ref[...] = _rms(x2, gf_ref[...])


def _combine(slots, top_w, x1, g2, gf, yb, seq_len):
    n, d = x1.shape
    n_seq = g2.shape[0]
    tc = _tile(n, 256)
    nt = n // tc
    vmem = 2 * TOP_K * tc * d * 4 + 4 * tc * d * 4 + 6 * tc * d * 4
    slots3 = slots.reshape(nt, 1, tc * TOP_K)
    return pl.pallas_call(
        functools.partial(_combine_kernel, tc=tc, seq_len=seq_len),
        grid=(nt,),
        in_specs=[pl.BlockSpec((None, 1, tc * TOP_K), lambda i: (i, 0, 0), memory_space=pltpu.SMEM),
                  pl.BlockSpec((None, 1, tc * TOP_K), lambda i: (jnp.minimum(i + 1, nt - 1), 0, 0),
                               memory_space=pltpu.SMEM),
                  pl.BlockSpec((tc, TOP_K), lambda i: (i, 0)),
                  pl.BlockSpec((tc, d), lambda i: (i, 0)),
                  _table_spec(n_seq, d, tc, seq_len),
                  pl.BlockSpec((1, d), lambda i: (0, 0)),
                  pl.BlockSpec(memory_space=pl.ANY)],
        out_specs=pl.BlockSpec((tc, d), lambda i: (i, 0)),
        out_shape=jax.ShapeDtypeStruct((n, d), F32),
        scratch_shapes=[pltpu.VMEM((2, TOP_K, tc, d), F32), pltpu.SemaphoreType.DMA((2,))],
        compiler_params=_params(("arbitrary",), vmem),
        name="combine",
    )(slots3, slots3, top_w, x1, g2, gf.reshape(1, d), yb)


def _layer(xp, xs, cp, cs, conv_state, gla_state, w):
    (w_ada, b_ada, norm1_g, w_in, w_dw, b_dw, conv_ln_g, conv_ln_b, w_alpha, b_alpha, gla_norm_g, w_out,
     norm2_g, w_router, b_router, w_gate, b_gate, w_up, b_up, w_down, b_down, final_g) = w
    bp, tp, d = xp.shape
    bs, ts, _ = xs.shape
    cc = w_dw.shape[1]
    qk = w_alpha.shape[1]
    n_exp = w_router.shape[1]
    n_main = 2 * cc + 2 * qk + 2 * cc
    np_, ns_ = bp * tp, bs * ts

    mod = _ada(jnp.concatenate([cp, cs], axis=0), w_ada, b_ada)
    mods_p = [mod[:bp, i * d:(i + 1) * d] for i in range(6)]
    mods_s = [mod[bp:, i * d:(i + 1) * d] for i in range(6)]
    w_in_t = w_in.T
    w_low = w_in_t[n_main:]
    w_main_bf = w_in_t[:n_main].astype(BF16)
    wo_bf = w_out.astype(BF16)

    xp2 = xp.reshape(np_, d)
    xs2 = xs.reshape(ns_, d)

    u_p, la_p = _in_proj(xp2, mods_p[0], mods_p[1], norm1_g, w_main_bf, w_low, w_alpha, b_alpha, tp)
    yc_p, conv_p = _conv_seq(u_p, bp, tp, w_dw, b_dw, conv_ln_g, conv_ln_b)
    yg_p, gla_p = _gla_seq(u_p, la_p, bp, tp, gla_norm_g, cc, qk)
    x1_p, h2_p, ti_p, tw_p = _out_proj(xp2, yc_p, yg_p, wo_bf, mods_p[2], mods_p[3], mods_p[4], norm2_g,
                                       w_router, b_router, tp)

    u_s, la_s = _in_proj(xs2, mods_s[0], mods_s[1], norm1_g, w_main_bf, w_low, w_alpha, b_alpha, ts)
    yc_s, conv_s = _conv_step(u_s, jnp.transpose(conv_state, (0, 2, 1, 3)), ts, w_dw, b_dw, conv_ln_g, conv_ln_b)
    conv_s = jnp.transpose(conv_s, (0, 2, 1, 3))
    yg_s, gla_s = _gla_step(u_s.reshape(bs, ts, n_main), la_s.reshape(bs, ts, qk), gla_state, gla_norm_g, cc, qk)
    x1_s, h2_s, ti_s, tw_s = _out_proj(xs2, yc_s, yg_s.reshape(ns_, cc), wo_bf, mods_s[2],
                                       mods_s[3], mods_s[4], norm2_g, w_router, b_router, ts)

    n_asg = (np_ + ns_) * TOP_K
    slots, cnt, off = _rank(jnp.concatenate([ti_p, ti_s], axis=0), n_exp, MOE_SUB)
    rows = -(-(n_asg + n_exp * MOE_SUB) // MOE_SUB) * MOE_SUB
    xsort = _dispatch(slots, h2_p, h2_s, cnt[0], off[0], rows, MOE_SUB)
    n_chunks = (n_asg + n_exp * MOE_SUB) // MOE_CHUNK + n_exp
    ch_e, ch_row, ch_n, tail = _chunk_table(cnt[0], off[0], n_chunks)
    yb = _moe(xsort, rows, ch_e, ch_row, ch_n, tail, w_gate, b_gate, w_up, b_up, w_down, b_down)

    y_p = _combine(slots[:np_], tw_p, x1_p, mods_p[5], final_g, yb, tp)
    y_s = _combine(slots[np_:], tw_s, x1_s, mods_s[5], final_g, yb, ts)
    return y_p.reshape(bp, tp, d), y_s.reshape(bs, ts, d), conv_p, gla_p, conv_s, gla_s


def kernel(x_prompt, x_sample, c_prompt, c_sample, state_conv, state_gla, w_ada, b_ada, norm1_g, w_in, w_dw, b_dw,
           conv_ln_g, conv_ln_b, w_alpha, b_alpha, gla_norm_g, w_out, norm2_g, w_router, b_router, w_gate, b_gate,
           w_up, b_up, w_down, b_down, final_norm_g):
    assert w_ada.shape[0] == 1, "the final norm is fused into the (single) layer"
    first = lambda a: a.reshape(a.shape[1:])
    w = tuple(first(a) for a in (w_ada, b_ada, norm1_g, w_in, w_dw, b_dw, conv_ln_g, conv_ln_b, w_alpha, b_alpha,
                                 gla_norm_g, w_out, norm2_g, w_router, b_router, w_gate, b_gate, w_up, b_up,
                                 w_down, b_down)) + (final_norm_g,)
    return _layer(x_prompt, x_sample, c_prompt, c_sample, state_conv, state_gla, w)
```
